```python
import jax, jax.numpy as jnp
from jax import lax
import numpy as np

D_MODEL = 1024
BATCH = 8
SEQ = 2048
DEPTH = 4
DEC_BATCH = 128
DEC_SEQ = 1
PAST_LEN = 16384
PAGE_SIZE = 128

GLA_HEADS = 4
GLA_DK = D_MODEL // (2 * GLA_HEADS)
GLA_DV = D_MODEL // GLA_HEADS
GLA_RANK = 16
GLA_GATE_NORM = 16.0
SCONV_DIM = D_MODEL
SCONV_WIDTH = 3
CONF_DIM = D_MODEL
CONF_WIDTH = 31
HGRN_EXPAND = 128
HGRN_HEADS = D_MODEL // HGRN_EXPAND
HGRN_DK = HGRN_EXPAND
HGRN_DV = D_MODEL // HGRN_HEADS
F_MIN = 1e-20
N_BRANCH = 4
CHUNK = 64
N_EXPERTS = 32
TOP_K = 4
D_FF = D_MODEL
SWIGLU_ALPHA = 1.702
SWIGLU_LIMIT = 7.0
MOE_BLOCK = 128
DEEPNORM_ALPHA = (2 * DEPTH) ** 0.25
DEEPNORM_BETA = (8 * DEPTH) ** -0.25
LN_EPS = 1e-5
RMS_EPS = 1e-6
IN_WIDTHS = (GLA_HEADS * GLA_DK, GLA_HEADS * GLA_DK, GLA_HEADS * GLA_DV, GLA_HEADS * GLA_DV, GLA_RANK,
             SCONV_DIM, SCONV_DIM, SCONV_DIM,
             2 * CONF_DIM,
             HGRN_HEADS * HGRN_DK, HGRN_HEADS * HGRN_DK, HGRN_HEADS * HGRN_DV, HGRN_HEADS * HGRN_DV,
             N_BRANCH * D_MODEL)
N_IN = sum(IN_WIDTHS)

kernel_name = 'hybrid_gla_conv_hgrn2_moe_deepnorm_step'


def layer_norm(x, g, b):
    xf = x.astype(jnp.float32)
    mu = jnp.mean(xf, axis=-1, keepdims=True)
    var = jnp.mean(jnp.square(xf - mu), axis=-1, keepdims=True)
    return ((xf - mu) * lax.rsqrt(var + LN_EPS) * g + b).astype(x.dtype)


def gated_rms_norm(o, gate, g):
    of = o.astype(jnp.float32)
    of = of * lax.rsqrt(jnp.mean(jnp.square(of), axis=-1, keepdims=True) + RMS_EPS) * g
    return (of * jax.nn.silu(gate.astype(jnp.float32))).astype(gate.dtype)


def split_cols(z, widths):
    return jnp.split(z, np.cumsum(widths)[:-1].tolist(), axis=-1)


def causal_dwconv(u, buf, w):
    full = jnp.concatenate([buf.astype(u.dtype), u], axis=1)
    y = lax.conv_general_dilated(full, w[:, None, :].astype(u.dtype), window_strides=(1,), padding='VALID',
                                 dimension_numbers=('NWC', 'WIO', 'NWC'), feature_group_count=u.shape[-1])
    return y, full[:, -(w.shape[0] - 1):]


def gated_linear_recurrence(q, k, v, log_a, s0):
    nb, t, nh, _ = q.shape
    dv = v.shape[-1]
    c = min(CHUNK, t)
    n = -(-t // c)
    pad = n * c - t

    def blocks(a):
        a = jnp.pad(a.astype(jnp.float32), ((0, 0), (0, pad), (0, 0), (0, 0)))
        return a.reshape(nb, n, c, nh, a.shape[-1]).transpose(1, 0, 3, 2, 4)

    causal = jnp.tril(jnp.ones((c, c), dtype=bool))[None, None, :, :, None]

    def step(S, blk):
        qc, kc, vc, ac = blk
        b = jnp.cumsum(ac, axis=2)
        b_end = b[:, :, -1:, :]
        diff = b[:, :, :, None, :] - b[:, :, None, :, :]
        decay = jnp.where(causal, jnp.exp(jnp.where(causal, diff, 0.0)), 0.0)
        scores = jnp.einsum('bhti,bhsi,bhtsi->bhts', qc, kc, decay)
        o = jnp.einsum('bhts,bhsj->bhtj', scores, vc) + jnp.einsum('bhti,bhij->bhtj', qc * jnp.exp(b), S)
        S = jnp.exp(b_end[:, :, 0, :, None]) * S + jnp.einsum('bhsi,bhsj->bhij', kc * jnp.exp(b_end - b), vc)
        return S, o

    S, o = lax.scan(step, s0.astype(jnp.float32), (blocks(q), blocks(k), blocks(v), blocks(log_a)))
    o = o.transpose(1, 0, 3, 2, 4).reshape(nb, n * c, nh, dv)[:, :t]
    return o, S.astype(s0.dtype)


def mixer_block(h, states, w):
    s_gla, s_hgrn, b_sconv, b_conf = states
    nb, t, _ = h.shape

    def heads(a, nh):
        return a.reshape(nb, t, nh, -1)

    z = h @ w['w_in']
    qa, ka, va, ga, lra, gate_b, gate_c, hb, glu, qd, fd, vd, gd, gates = split_cols(z, IN_WIDTHS)

    log_alpha = jax.nn.log_sigmoid((lra @ w['w_gla_lr'] + w['b_gla_lr']).astype(jnp.float32)) / GLA_GATE_NORM
    o_a, s_gla = gated_linear_recurrence(heads(qa, GLA_HEADS) * GLA_DK ** -0.5, heads(ka, GLA_HEADS),
                                         heads(va, GLA_HEADS), heads(log_alpha, GLA_HEADS), s_gla)
    o_a = gated_rms_norm(o_a, heads(ga, GLA_HEADS), w['gla_norm_g']).reshape(nb, t, -1)
    br_a = o_a @ w['w_br_a']

    y_b, b_sconv = causal_dwconv(gate_c * hb, b_sconv, w['sconv_w'])
    br_b = (gate_b * y_b) @ w['w_br_b']

    u_c = glu[..., :CONF_DIM] * jax.nn.sigmoid(glu[..., CONF_DIM:])
    y_c, b_conf = causal_dwconv(u_c, b_conf, w['conf_conv_w'])
    y_c = jax.nn.silu(layer_norm(y_c + w['conf_conv_b'], w['conf_ln_g'], w['conf_ln_b']))
    br_c = y_c @ w['w_br_c']

    lb = w['hgrn_lb']
    f_d = lb + (1.0 - lb) * jax.nn.sigmoid(fd.astype(jnp.float32))
    log_f = jnp.log(jnp.maximum(f_d, F_MIN))
    k_d = 1.0 - f_d
    o_d, s_hgrn = gated_linear_recurrence(heads(jax.nn.silu(qd), HGRN_HEADS), heads(k_d, HGRN_HEADS),
                                          heads(vd, HGRN_HEADS), heads(log_f, HGRN_HEADS), s_hgrn)
    o_d = gated_rms_norm(o_d, heads(gd, HGRN_HEADS), w['hgrn_norm_g']).reshape(nb, t, -1)
    br_d = o_d @ w['w_br_d']

    gate = jax.nn.sigmoid(gates.astype(jnp.float32)).reshape(nb, t, N_BRANCH, D_MODEL)
    merged = gate[:, :, 0] * br_a + gate[:, :, 1] * br_b + gate[:, :, 2] * br_c + gate[:, :, 3] * br_d
    out = merged.astype(h.dtype) @ w['w_o']
    return out, (s_gla, s_hgrn, b_sconv, b_conf)


def moe_block(h, w):
    nb, t, d = h.shape
    x = h.reshape(-1, d)
    n_tok = x.shape[0]
    n_assign = n_tok * TOP_K
    logits = (x @ w['w_router'] + w['b_router']).astype(jnp.float32)
    top_v, top_i = lax.top_k(logits, TOP_K)
    top_w = jax.nn.softmax(top_v, axis=-1)
    flat_e = top_i.reshape(-1)
    order = jnp.argsort(flat_e)
    e_sorted = flat_e[order]
    counts = jnp.zeros((N_EXPERTS,), jnp.int32).at[flat_e].add(1)
    padded = (counts + MOE_BLOCK - 1) // MOE_BLOCK * MOE_BLOCK
    start = jnp.cumsum(counts) - counts
    p_end = jnp.cumsum(padded)
    p_start = p_end - padded
    dest = p_start[e_sorted] + jnp.arange(n_assign, dtype=jnp.int32) - start[e_sorted]
    n_blocks = -(-n_assign // MOE_BLOCK) + N_EXPERTS
    rows = n_blocks * MOE_BLOCK
    slot = jnp.full((rows,), n_assign, jnp.int32).at[dest].set(order.astype(jnp.int32))
    valid = slot < n_assign
    slot_tok = jnp.where(valid, slot // TOP_K, 0)
    slot_w = jnp.where(valid, top_w.reshape(-1)[jnp.minimum(slot, n_assign - 1)], 0.0)
    block_e = jnp.minimum(jnp.searchsorted(p_end, jnp.arange(n_blocks, dtype=jnp.int32) * MOE_BLOCK, side='right'),
                          N_EXPERTS - 1)
    xb = x[slot_tok].reshape(n_blocks, MOE_BLOCK, d)

    def expert_block(args):
        xi, e = args
        gu = xi @ w['w_gate_up'][e] + w['b_gate_up'][e]
        x_glu = jnp.minimum(gu[:, :D_FF], SWIGLU_LIMIT)
        x_lin = jnp.clip(gu[:, D_FF:], -SWIGLU_LIMIT, SWIGLU_LIMIT)
        act = x_glu * jax.nn.sigmoid(SWIGLU_ALPHA * x_glu) * (x_lin + 1.0)
        return act @ w['w_down'][e] + w['b_down'][e]

    yb = lax.map(expert_block, (xb, block_e)).reshape(rows, d)
    y = (yb * slot_w[:, None]).astype(x.dtype)
    out = jnp.zeros_like(x).at[slot_tok].add(y)
    return out.reshape(nb, t, d)


def trunk(x, states, ln_in_g, ln_in_b, params):
    h = layer_norm(x, ln_in_g, ln_in_b)
    new = ([], [], [], [])
    for l in range(DEPTH):
        w = {name: p[l] for name, p in params.items()}
        mix, st = mixer_block(h, tuple(s[l] for s in states), w)
        h = layer_norm(DEEPNORM_ALPHA * h + mix, w['ln1_g'], w['ln1_b'])
        h = layer_norm(DEEPNORM_ALPHA * h + moe_block(h, w), w['ln2_g'], w['ln2_b'])
        for acc, s in zip(new, st):
            acc.append(s)
    return h, tuple(jnp.stack(a) for a in new)


def setup_inputs(seed: int = 0) -> dict:
    key = jax.random.key(seed)
    keys = iter(jax.random.split(key, 48))

    def nrm(shape, scale):
        return jax.random.normal(next(keys), shape, jnp.float32) * scale

    L, D = DEPTH, D_MODEL
    return {
        'x_prompt': nrm((BATCH, SEQ, D), 1.0),
        'x_sample': nrm((DEC_BATCH, DEC_SEQ, D), 1.0),
        'state_gla': nrm((L, DEC_BATCH, GLA_HEADS, GLA_DK, GLA_DV), 0.5),
        'state_hgrn': nrm((L, DEC_BATCH, HGRN_HEADS, HGRN_DK, HGRN_DV), 0.3),
        'cache_sconv': nrm((L, DEC_BATCH, SCONV_WIDTH - 1, SCONV_DIM), 0.5),
        'cache_conformer': nrm((L, DEC_BATCH, CONF_WIDTH - 1, CONF_DIM), 0.5),
        'ln_in_g': 1.0 + nrm((D,), 0.1),
        'ln_in_b': nrm((D,), 0.02),
        'w_in': nrm((L, D, N_IN), D ** -0.5),
        'w_gla_lr': nrm((L, GLA_RANK, GLA_HEADS * GLA_DK), GLA_RANK ** -0.5),
        'b_gla_lr': nrm((L, GLA_HEADS * GLA_DK), 0.1),
        'gla_norm_g': 1.0 + nrm((L, GLA_DV), 0.1),
        'w_br_a': nrm((L, GLA_HEADS * GLA_DV, D), (GLA_HEADS * GLA_DV) ** -0.5),
        'sconv_w': nrm((L, SCONV_WIDTH, SCONV_DIM), SCONV_WIDTH ** -0.5),
        'w_br_b': nrm((L, SCONV_DIM, D), SCONV_DIM ** -0.5),
        'conf_conv_w': nrm((L, CONF_WIDTH, CONF_DIM), CONF_WIDTH ** -0.5),
        'conf_conv_b': nrm((L, CONF_DIM), 0.02),
        'conf_ln_g': 1.0 + nrm((L, CONF_DIM), 0.1),
        'conf_ln_b': nrm((L, CONF_DIM), 0.02),
        'w_br_c': nrm((L, CONF_DIM, D), CONF_DIM ** -0.5),
        'hgrn_lb_logits': nrm((L, HGRN_HEADS * HGRN_DK), 1.0),
        'hgrn_norm_g': 1.0 + nrm((L, HGRN_DV), 0.1),
        'w_br_d': nrm((L, HGRN_HEADS * HGRN_DV, D), (HGRN_HEADS * HGRN_DV) ** -0.5),
        'w_o': nrm((L, D, D), D ** -0.5 * DEEPNORM_BETA),
        'ln1_g': 1.0 + nrm((L, D), 0.1),
        'ln1_b': nrm((L, D), 0.02),
        'w_router': nrm((L, D, N_EXPERTS), D ** -0.5),
        'b_router': nrm((L, N_EXPERTS), 0.01),
        'w_gate_up': nrm((L, N_EXPERTS, D, 2 * D_FF), D ** -0.5),
        'b_gate_up': nrm((L, N_EXPERTS, 2 * D_FF), 0.02),
        'w_down': nrm((L, N_EXPERTS, D_FF, D), D_FF ** -0.5 * DEEPNORM_BETA),
        'b_down': nrm((L, N_EXPERTS, D), 0.02),
        'ln2_g': 1.0 + nrm((L, D), 0.1),
        'ln2_b': nrm((L, D), 0.02),
    }


def reference(x_prompt, x_sample, state_gla, state_hgrn, cache_sconv, cache_conformer,
              ln_in_g, ln_in_b, w_in, w_gla_lr, b_gla_lr, gla_norm_g, w_br_a, sconv_w, w_br_b,
              conf_conv_w, conf_conv_b, conf_ln_g, conf_ln_b, w_br_c, hgrn_lb_logits, hgrn_norm_g, w_br_d,
              w_o, ln1_g, ln1_b, w_router, b_router, w_gate_up, b_gate_up, w_down, b_down, ln2_g, ln2_b):
    lb_p = jax.nn.softmax(hgrn_lb_logits.astype(jnp.float32), axis=0)
    hgrn_lb = jnp.cumsum(lb_p, axis=0) - lb_p[:1]
    params = {
        'w_in': w_in, 'w_gla_lr': w_gla_lr, 'b_gla_lr': b_gla_lr, 'gla_norm_g': gla_norm_g, 'w_br_a': w_br_a,
        'sconv_w': sconv_w, 'w_br_b': w_br_b,
        'conf_conv_w': conf_conv_w, 'conf_conv_b': conf_conv_b, 'conf_ln_g': conf_ln_g, 'conf_ln_b': conf_ln_b,
        'w_br_c': w_br_c, 'hgrn_lb': hgrn_lb, 'hgrn_norm_g': hgrn_norm_g, 'w_br_d': w_br_d,
        'w_o': w_o, 'ln1_g': ln1_g, 'ln1_b': ln1_b,
        'w_router': w_router, 'b_router': b_router, 'w_gate_up': w_gate_up, 'b_gate_up': b_gate_up,
        'w_down': w_down, 'b_down': b_down, 'ln2_g': ln2_g, 'ln2_b': ln2_b,
    }
    nbp = x_prompt.shape[0]
    dt = x_prompt.dtype
    zero_states = (jnp.zeros((DEPTH, nbp, GLA_HEADS, GLA_DK, GLA_DV), dt),
                   jnp.zeros((DEPTH, nbp, HGRN_HEADS, HGRN_DK, HGRN_DV), dt),
                   jnp.zeros((DEPTH, nbp, SCONV_WIDTH - 1, SCONV_DIM), dt),
                   jnp.zeros((DEPTH, nbp, CONF_WIDTH - 1, CONF_DIM), dt))
    y_prompt, (gla_p, hgrn_p, sconv_p, conf_p) = trunk(x_prompt, zero_states, ln_in_g, ln_in_b, params)
    y_sample, (gla_s, hgrn_s, sconv_s, conf_s) = trunk(
        x_sample, (state_gla, state_hgrn, cache_sconv, cache_conformer), ln_in_g, ln_in_b, params)
    return (y_prompt, y_sample, gla_p, hgrn_p, sconv_p, conf_p, gla_s, hgrn_s, sconv_s, conf_s)
```

```python
import functools

import jax
import jax.numpy as jnp
from jax import lax
from jax.experimental import pallas as pl
from jax.experimental.pallas import tpu as pltpu

F32 = jnp.float32
BF16 = jnp.bfloat16

TOP_K = 4
CHUNK = 64
SUB = 16
GLA_GATE_NORM = 16.0
F_MIN = 1e-20
LN_EPS = 1e-5
RMS_EPS = 1e-6
SWIGLU_ALPHA = 1.702
SWIGLU_LIMIT = 7.0
MOE_ROWS = 256
LANES = 128
CONV_ROWS = 16
CONF_HIST = 32
SCONV_HIST = 8
V7X_VMEM_LIMIT = 56 * 1024 * 1024


def _params(*sem):
    return pltpu.CompilerParams(dimension_semantics=sem, vmem_limit_bytes=V7X_VMEM_LIMIT)


def _tile(n, pref, mult=8):
    if n <= pref:
        return n
    t = pref - pref % mult
    while t >= mult:
        if n % t == 0:
            return t
        t -= mult
    raise ValueError(f"no tile for {n}")


def _sigmoid(x):
    return 1.0 / (1.0 + jnp.exp(-x))


def _silu(x):
    return x * _sigmoid(x)


def _log_sigmoid(x):
    return jnp.minimum(x, 0.0) - jnp.log(1.0 + jnp.exp(-jnp.abs(x)))


def _layer_norm(x, g, b):
    mu = jnp.mean(x, axis=-1, keepdims=True)
    xc = x - mu
    var = jnp.mean(xc * xc, axis=-1, keepdims=True)
    return xc * lax.rsqrt(var + LN_EPS) * g + b


def _dot(a, b):
    return jnp.dot(a, b, preferred_element_type=F32)


def _dot_nt(a, b):
    return lax.dot_general(a, b, (((1,), (1,)), ((), ())), preferred_element_type=F32)


def _dot_tn(a, b):
    return lax.dot_general(a, b, (((0,), (0,)), ((), ())), preferred_element_type=F32)


def _split3(x):
    hi = x.astype(BF16)
    r = x - hi.astype(F32)
    mid = r.astype(BF16)
    lo = (r - mid.astype(F32)).astype(BF16)
    return hi, mid, lo


def _ln_kernel(x_ref, g_ref, b_ref, o_ref, ob_ref):
    y = _layer_norm(x_ref[...], g_ref[...], b_ref[...])
    o_ref[...] = y
    ob_ref[...] = y.astype(BF16)


def _ln_call(x, g, b):
    m, d = x.shape
    tm = _tile(m, 512, 16)
    row = pl.BlockSpec((tm, d), lambda i: (i, 0))
    vec = pl.BlockSpec((1, d), lambda i: (0, 0))
    return pl.pallas_call(
        _ln_kernel, grid=(m // tm,), in_specs=[row, vec, vec], out_specs=[row, row],
        out_shape=[jax.ShapeDtypeStruct((m, d), F32), jax.ShapeDtypeStruct((m, d), BF16)],
        compiler_params=_params("parallel"), name="ln_in")(x, g.reshape(1, d), b.reshape(1, d))


def _mm_kernel(x_ref, w_ref, o_ref):
    o_ref[...] = _dot(x_ref[...], w_ref[...])


def _matmul(x, w):
    m, k = x.shape
    n = w.shape[1]
    tm = _tile(m, 1024, 16)
    tn = _tile(n, 2048, LANES)
    return pl.pallas_call(
        _mm_kernel, grid=(n // tn, m // tm),
        in_specs=[pl.BlockSpec((tm, k), lambda j, i: (i, 0)), pl.BlockSpec((k, tn), lambda j, i: (0, j))],
        out_specs=pl.BlockSpec((tm, tn), lambda j, i: (i, j)),
        out_shape=jax.ShapeDtypeStruct((m, n), F32),
        compiler_params=_params("parallel", "parallel"), name="in_proj")(x, w)


def _gla_decay_kernel(h_ref, w1_ref, w2_ref, b2_ref, o_ref):
    lra = _dot(h_ref[...], w1_ref[...])
    x = _dot(lra.astype(BF16), w2_ref[...]) + b2_ref[...]
    o_ref[...] = _log_sigmoid(x) * (1.0 / GLA_GATE_NORM)


def _gla_decay(hb, w1, w2, b2):
    m, d = hb.shape
    n = w2.shape[1]
    tm = _tile(m, 512, 16)
    return pl.pallas_call(
        _gla_decay_kernel, grid=(m // tm,),
        in_specs=[pl.BlockSpec((tm, d), lambda i: (i, 0)), pl.BlockSpec(w1.shape, lambda i: (0, 0)),
                  pl.BlockSpec(w2.shape, lambda i: (0, 0)), pl.BlockSpec((1, n), lambda i: (0, 0))],
        out_specs=pl.BlockSpec((tm, n), lambda i: (i, 0)),
        out_shape=jax.ShapeDtypeStruct((m, n), F32),
        compiler_params=_params("parallel"), name="gla_decay")(hb, w1, w2, b2)


def _cumsum_rows(x):
    c = x.shape[0]
    r = lax.broadcasted_iota(jnp.int32, (c, c), 0)
    s = lax.broadcasted_iota(jnp.int32, (c, c), 1)
    tri = jnp.where(r >= s, 1.0, 0.0).astype(BF16)
    hi, mid, lo = _split3(x)
    return _dot(tri, hi) + _dot(tri, mid) + _dot(tri, lo)


def _intra_scores(q, k, b):
    c = q.shape[0]
    lane = lax.broadcasted_iota(jnp.int32, (SUB, c), 1)
    row = lax.broadcasted_iota(jnp.int32, (SUB, c), 0)
    blocks = []
    for i in range(c // SUB):
        lo = i * SUB
        qi, ki, bi = q[lo:lo + SUB], k[lo:lo + SUB], b[lo:lo + SUB]
        if i > 0:
            ref = b[lo - 1:lo]
            qs = qi * jnp.exp(bi - ref)
            ks = k * jnp.exp(jnp.minimum(ref - b, 0.0))
            a = jnp.where(lane < lo, _dot_nt(qs.astype(BF16), ks.astype(BF16)), 0.0)
        else:
            a = jnp.zeros((SUB, c), F32)
        for s in range(SUB):
            p = qi * ki[s:s + 1] * jnp.exp(jnp.minimum(bi - bi[s:s + 1], 0.0))
            col = jnp.sum(p, axis=-1, keepdims=True)
            a = jnp.where((lane == lo + s) & (row >= s), col, a)
        blocks.append(a)
    return jnp.concatenate(blocks, axis=0)


def _chunk_step(q, k, v, b, st):
    c = q.shape[0]
    b_end = b[c - 1:c]
    o = _dot_nt((q * jnp.exp(b)).astype(BF16), st.astype(BF16))
    a = _intra_scores(q, k, b)
    vb = v.astype(BF16)
    o = o + _dot(a.astype(BF16), vb)
    kb = (k * jnp.exp(b_end - b)).astype(BF16)
    st_new = st * jnp.exp(b_end) + _dot_tn(vb, kb)
    return o, st_new


def _gated_rms(o, gate, g):
    o = o * lax.rsqrt(jnp.mean(o * o, axis=-1, keepdims=True) + RMS_EPS) * g
    return o * _silu(gate)


def _gla_kernel(q_ref, k_ref, v_ref, g_ref, la_ref, ng_ref, o_ref, so_ref, st_ref, *, heads, dk, dv):
    j = pl.program_id(1)

    @pl.when(j == 0)
    def _():
        st_ref[...] = jnp.zeros_like(st_ref)

    b_all = _cumsum_rows(la_ref[...])
    scale = dk ** -0.5
    for h in range(heads):
        ks = slice(h * dk, (h + 1) * dk)
        vs = slice(h * dv, (h + 1) * dv)
        o, st = _chunk_step(q_ref[:, ks] * scale, k_ref[:, ks], v_ref[:, vs], b_all[:, ks], st_ref[h])
        st_ref[h] = st
        o_ref[:, vs] = _gated_rms(o, g_ref[:, vs], ng_ref[...]).astype(BF16)

    @pl.when(j == pl.num_programs(1) - 1)
    def _():
        for h in range(heads):
            so_ref[0, h] = st_ref[h].T


def _gla_prompt(z, la, norm_g, nb, t, heads, dk, dv):
    c = min(CHUNK, t)
    n = t // c
    wk, wv = heads * dk, heads * dv
    row = lambda col: (lambda bi, j: (bi * n + j, col))
    kern = functools.partial(_gla_kernel, heads=heads, dk=dk, dv=dv)
    return pl.pallas_call(
        kern, grid=(nb, n),
        in_specs=[pl.BlockSpec((c, wk), row(0)), pl.BlockSpec((c, wk), row(1)),
                  pl.BlockSpec((c, wv), row(1)), pl.BlockSpec((c, wv), row(2)),
                  pl.BlockSpec((c, wk), row(0)), pl.BlockSpec((1, dv), lambda bi, j: (0, 0))],
        out_specs=[pl.BlockSpec((c, wv), row(0)),
                   pl.BlockSpec((1, heads, dk, dv), lambda bi, j: (bi, 0, 0, 0))],
        out_shape=[jax.ShapeDtypeStruct((nb * t, wv), BF16),
                   jax.ShapeDtypeStruct((nb, heads, dk, dv), F32)],
        scratch_shapes=[pltpu.VMEM((heads, dv, dk), F32)],
        compiler_params=_params("parallel", "arbitrary"), name="gla_prompt")(z, z, z, z, la, norm_g)


def _hgrn_inputs(qd, fd, lb):
    f = lb + (1.0 - lb) * _sigmoid(fd)
    return _silu(qd), 1.0 - f, jnp.log(jnp.maximum(f, F_MIN))


def _hgrn_kernel(q_ref, f_ref, v_ref, g_ref, lb_ref, ng_ref, o_ref, so_ref, st_ref, *, heads, dk, dv):
    j = pl.program_id(1)

    @pl.when(j == 0)
    def _():
        st_ref[...] = jnp.zeros_like(st_ref)

    q_all, k_all, lf = _hgrn_inputs(q_ref[...], f_ref[...], lb_ref[...])
    b_all = _cumsum_rows(lf)
    for h in range(heads):
        ks = slice(h * dk, (h + 1) * dk)
        vs = slice(h * dv, (h + 1) * dv)
        o, st = _chunk_step(q_all[:, ks], k_all[:, ks], v_ref[:, vs], b_all[:, ks], st_ref[h])
        st_ref[h] = st
        o_ref[:, vs] = _gated_rms(o, g_ref[:, vs], ng_ref[...]).astype(BF16)

    @pl.when(j == pl.num_programs(1) - 1)
    def _():
        for h in range(heads):
            so_ref[0, h] = st_ref[h].T


def _hgrn_prompt(z, lb, norm_g, nb, t, heads, dk, dv, col0):
    c = min(CHUNK, t)
    n = t // c
    w = heads * dk
    row = lambda col: (lambda bi, j: (bi * n + j, col0 + col))
    kern = functools.partial(_hgrn_kernel, heads=heads, dk=dk, dv=dv)
    return pl.pallas_call(
        kern, grid=(nb, n),
        in_specs=[pl.BlockSpec((c, w), row(0)), pl.BlockSpec((c, w), row(1)),
                  pl.BlockSpec((c, w), row(2)), pl.BlockSpec((c, w), row(3)),
                  pl.BlockSpec((1, w), lambda bi, j: (0, 0)), pl.BlockSpec((1, dv), lambda bi, j: (0, 0))],
        out_specs=[pl.BlockSpec((c, w), lambda bi, j: (bi * n + j, 0)),
                   pl.BlockSpec((1, heads, dk, dv), lambda bi, j: (bi, 0, 0, 0))],
        out_shape=[jax.ShapeDtypeStruct((nb * t, w), BF16),
                   jax.ShapeDtypeStruct((nb, heads, dk, dv), F32)],
        scratch_shapes=[pltpu.VMEM((heads, dv, dk), F32)],
        compiler_params=_params("parallel", "arbitrary"), name="hgrn_prompt")(z, z, z, z, lb, norm_g)


def _columns(x, n):
    w = x.shape[1]
    pad = jnp.concatenate([x, jnp.zeros((w - n, w), F32)], axis=0) if n < w else x
    return pad.T


def _step_heads(q_all, k_all, a_all, v_ref, g_ref, ng_ref, s_ref, o_ref, so_ref, *, heads, dk, dv, tb):
    for h in range(heads):
        ks = slice(h * dk, (h + 1) * dk)
        vs = slice(h * dv, (h + 1) * dv)
        qc, kc, ac = _columns(q_all[:, ks], tb), _columns(k_all[:, ks], tb), _columns(a_all[:, ks], tb)
        v = v_ref[:, vs]
        row = lax.broadcasted_iota(jnp.int32, (tb, dv), 0)
        o = jnp.zeros((tb, dv), F32)
        for n in range(tb):
            s_new = ac[:, n:n + 1] * s_ref[n, h] + kc[:, n:n + 1] * v[n:n + 1]
            so_ref[n, h] = s_new
            o = jnp.where(row == n, jnp.sum(qc[:, n:n + 1] * s_new, axis=0, keepdims=True), o)
        o_ref[:, vs] = _gated_rms(o, g_ref[:, vs], ng_ref[...]).astype(BF16)


def _gla_step_kernel(q_ref, k_ref, v_ref, g_ref, la_ref, ng_ref, s_ref, o_ref, so_ref, *, heads, dk, dv, tb):
    _step_heads(q_ref[...] * dk ** -0.5, k_ref[...], jnp.exp(la_ref[...]), v_ref, g_ref, ng_ref, s_ref,
                o_ref, so_ref, heads=heads, dk=dk, dv=dv, tb=tb)


def _hgrn_step_kernel(q_ref, f_ref, v_ref, g_ref, lb_ref, ng_ref, s_ref, o_ref, so_ref, *, heads, dk, dv, tb):
    q_all, k_all, lf = _hgrn_inputs(q_ref[...], f_ref[...], lb_ref[...])
    _step_heads(q_all, k_all, jnp.exp(lf), v_ref, g_ref, ng_ref, s_ref, o_ref, so_ref,
                heads=heads, dk=dk, dv=dv, tb=tb)


def _gla_step(z, la, norm_g, state, heads, dk, dv):
    m = z.shape[0]
    tb = 8
    wk, wv = heads * dk, heads * dv
    row = lambda col: (lambda i: (i, col))
    st = pl.BlockSpec((tb, heads, dk, dv), lambda i: (i, 0, 0, 0))
    kern = functools.partial(_gla_step_kernel, heads=heads, dk=dk, dv=dv, tb=tb)
    return pl.pallas_call(
        kern, grid=(m // tb,),
        in_specs=[pl.BlockSpec((tb, wk), row(0)), pl.BlockSpec((tb, wk), row(1)),
                  pl.BlockSpec((tb, wv), row(1)), pl.BlockSpec((tb, wv), row(2)),
                  pl.BlockSpec((tb, wk), row(0)), pl.BlockSpec((1, dv), lambda i: (0, 0)), st],
        out_specs=[pl.BlockSpec((tb, wv), row(0)), st],
        out_shape=[jax.ShapeDtypeStruct((m, wv), BF16), jax.ShapeDtypeStruct(state.shape, F32)],
        compiler_params=_params("parallel"), name="gla_step")(z, z, z, z, la, norm_g, state)


def _hgrn_step(z, lb, norm_g, state, heads, dk, dv, col0):
    m = z.shape[0]
    tb = 8
    w = heads * dk
    row = lambda col: (lambda i: (i, col0 + col))
    st = pl.BlockSpec((tb, heads, dk, dv), lambda i: (i, 0, 0, 0))
    kern = functools.partial(_hgrn_step_kernel, heads=heads, dk=dk, dv=dv, tb=tb)
    return pl.pallas_call(
        kern, grid=(m // tb,),
        in_specs=[pl.BlockSpec((tb, w), row(0)), pl.BlockSpec((tb, w), row(1)),
                  pl.BlockSpec((tb, w), row(2)), pl.BlockSpec((tb, w), row(3)),
                  pl.BlockSpec((1, w), lambda i: (0, 0)), pl.BlockSpec((1, dv), lambda i: (0, 0)), st],
        out_specs=[pl.BlockSpec((tb, w), lambda i: (i, 0)), st],
        out_shape=[jax.ShapeDtypeStruct((m, w), BF16), jax.ShapeDtypeStruct(state.shape, F32)],
        compiler_params=_params("parallel"), name="hgrn_step")(z, z, z, z, lb, norm_g, state)


def _conv_kernel(gb_ref, gc_ref, hb_ref, ga_ref, gs_ref, sw_ref, cw_ref, cb_ref, lg_ref, lb_ref,
                 ob_ref, oc_ref, so_ref, co_ref, ubuf, cbuf, *, tt, sw, cw):
    t = pl.program_id(1)

    @pl.when(t == 0)
    def _():
        ubuf[0:SCONV_HIST] = jnp.zeros((SCONV_HIST, ubuf.shape[1]), F32)
        cbuf[0:CONF_HIST] = jnp.zeros((CONF_HIST, cbuf.shape[1]), F32)

    ubuf[SCONV_HIST:SCONV_HIST + tt] = gc_ref[...] * hb_ref[...]
    cbuf[CONF_HIST:CONF_HIST + tt] = ga_ref[...] * _sigmoid(gs_ref[...])
    s0 = SCONV_HIST - (sw - 1)
    c0 = CONF_HIST - (cw - 1)

    def strip(r, carry):
        r0 = pl.multiple_of(r * CONV_ROWS, CONV_ROWS)
        win = ubuf[pl.ds(r0, CONV_ROWS + SCONV_HIST)]
        acc = sw_ref[0:1] * win[s0:s0 + CONV_ROWS]
        for j in range(1, sw):
            acc = acc + sw_ref[j:j + 1] * win[s0 + j:s0 + j + CONV_ROWS]
        ob_ref[pl.ds(r0, CONV_ROWS)] = (gb_ref[pl.ds(r0, CONV_ROWS)] * acc).astype(BF16)
        win = cbuf[pl.ds(r0, CONV_ROWS + CONF_HIST)]
        acc = cw_ref[0:1] * win[c0:c0 + CONV_ROWS]
        for j in range(1, cw):
            acc = acc + cw_ref[j:j + 1] * win[c0 + j:c0 + j + CONV_ROWS]
        y = _layer_norm(acc + cb_ref[...], lg_ref[...], lb_ref[...])
        oc_ref[pl.ds(r0, CONV_ROWS)] = _silu(y).astype(BF16)
        return carry

    lax.fori_loop(0, tt // CONV_ROWS, strip, 0)

    @pl.when(t == pl.num_programs(1) - 1)
    def _():
        so_ref[0] = ubuf[SCONV_HIST + tt - (sw - 1):SCONV_HIST + tt]
        co_ref[0] = cbuf[CONF_HIST + tt - (cw - 1):CONF_HIST + tt]

    ubuf[0:SCONV_HIST] = ubuf[tt:tt + SCONV_HIST]
    cbuf[0:CONF_HIST] = cbuf[tt:tt + CONF_HIST]


def _conv_prompt(z, sconv_w, conf_w, conf_b, ln_g, ln_b, nb, t, d, col0):
    sw, cw = sconv_w.shape[0], conf_w.shape[0]
    tt = _tile(t, 256, CONF_HIST)
    n = t // tt
    row = lambda col: (lambda bi, j: (bi * n + j, col0 + col))
    vec = lambda r: pl.BlockSpec((r, d), lambda bi, j: (0, 0))
    kern = functools.partial(_conv_kernel, tt=tt, sw=sw, cw=cw)
    blk = lambda col: pl.BlockSpec((tt, d), row(col))
    return pl.pallas_call(
        kern, grid=(nb, n),
        in_specs=[blk(0), blk(1), blk(2), blk(3), blk(4), vec(sw), vec(cw), vec(1), vec(1), vec(1)],
        out_specs=[pl.BlockSpec((tt, d), lambda bi, j: (bi * n + j, 0)),
                   pl.BlockSpec((tt, d), lambda bi, j: (bi * n + j, 0)),
                   pl.BlockSpec((1, sw - 1, d), lambda bi, j: (bi, 0, 0)),
                   pl.BlockSpec((1, cw - 1, d), lambda bi, j: (bi, 0, 0))],
        out_shape=[jax.ShapeDtypeStruct((nb * t, d), BF16), jax.ShapeDtypeStruct((nb * t, d), BF16),
                   jax.ShapeDtypeStruct((nb, sw - 1, d), F32), jax.ShapeDtypeStruct((nb, cw - 1, d), F32)],
        scratch_shapes=[pltpu.VMEM((SCONV_HIST + tt, d), F32), pltpu.VMEM((CONF_HIST + tt, d), F32)],
        compiler_params=_params("parallel", "arbitrary"), name="conv_prompt")(
            z, z, z, z, z, sconv_w, conf_w, conf_b, ln_g, ln_b)


def _conv_step_kernel(gb_ref, gc_ref, hb_ref, ga_ref, gs_ref, sc_ref, cc_ref, sw_ref, cw_ref, cb_ref,
                      lg_ref, lb_ref, ob_ref, oc_ref, u_ref, uc_ref, *, tb, sw, cw):
    u = gc_ref[...] * hb_ref[...]
    uc = ga_ref[...] * _sigmoid(gs_ref[...])
    u_ref[...] = u
    uc_ref[...] = uc
    row = lax.broadcasted_iota(jnp.int32, u.shape, 0)
    ys = jnp.zeros_like(u)
    yc = jnp.zeros_like(u)
    for n in range(tb):
        ys = jnp.where(row == n, jnp.sum(sc_ref[n] * sw_ref[0:sw - 1], axis=0, keepdims=True), ys)
        yc = jnp.where(row == n, jnp.sum(cc_ref[n] * cw_ref[0:cw - 1], axis=0, keepdims=True), yc)
    yb = ys + sw_ref[sw - 1:sw] * u
    ob_ref[...] = (gb_ref[...] * yb).astype(BF16)
    y = yc + cw_ref[cw - 1:cw] * uc + cb_ref[...]
    oc_ref[...] = _silu(_layer_norm(y, lg_ref[...], lb_ref[...])).astype(BF16)


def _conv_step(z, cache_s, cache_c, sconv_w, conf_w, conf_b, ln_g, ln_b, d, col0):
    m = z.shape[0]
    sw, cw = sconv_w.shape[0], conf_w.shape[0]
    tb = 16
    blk = lambda col: pl.BlockSpec((tb, d), lambda i: (i, col0 + col))
    vec = lambda r: pl.BlockSpec((r, d), lambda i: (0, 0))
    out = pl.BlockSpec((tb, d), lambda i: (i, 0))
    kern = functools.partial(_conv_step_kernel, tb=tb, sw=sw, cw=cw)
    return pl.pallas_call(
        kern, grid=(m // tb,),
        in_specs=[blk(0), blk(1), blk(2), blk(3), blk(4),
                  pl.BlockSpec((tb, sw - 1, d), lambda i: (i, 0, 0)),
                  pl.BlockSpec((tb, cw - 1, d), lambda i: (i, 0, 0)),
                  vec(sw), vec(cw), vec(1), vec(1), vec(1)],
        out_specs=[out, out, out, out],
        out_shape=[jax.ShapeDtypeStruct((m, d), BF16), jax.ShapeDtypeStruct((m, d), BF16),
                   jax.ShapeDtypeStruct((m, d), F32), jax.ShapeDtypeStruct((m, d), F32)],
        compiler_params=_params("parallel"), name="conv_step")(
            z, z, z, z, z, cache_s, cache_c, sconv_w, conf_w, conf_b, ln_g, ln_b)


def _merge_kernel(oa_ref, ob_ref, oc_ref, od_ref, g0_ref, g1_ref, g2_ref, g3_ref, h_ref,
                  wa_ref, wb_ref, wc_ref, wd_ref, wo_ref, lg_ref, lb_ref, o_ref, *, alpha):
    merged = _sigmoid(g0_ref[...]) * _dot(oa_ref[...], wa_ref[...])
    merged = merged + _sigmoid(g1_ref[...]) * _dot(ob_ref[...], wb_ref[...])
    merged = merged + _sigmoid(g2_ref[...]) * _dot(oc_ref[...], wc_ref[...])
    merged = merged + _sigmoid(g3_ref[...]) * _dot(od_ref[...], wd_ref[...])
    out = _dot(merged.astype(BF16), wo_ref[...])
    o_ref[...] = _layer_norm(alpha * h_ref[...] + out, lg_ref[...], lb_ref[...])


def _merge(oa, ob, oc, od, z, h, wa, wb, wc, wd, wo, ln_g, ln_b, alpha, gate_col0):
    m, d = h.shape
    tm = _tile(m, 256, 16)
    row = pl.BlockSpec((tm, d), lambda i: (i, 0))
    gate = lambda c: pl.BlockSpec((tm, d), lambda i: (i, gate_col0 + c))
    wsp = pl.BlockSpec((d, d), lambda i: (0, 0))
    vec = pl.BlockSpec((1, d), lambda i: (0, 0))
    return pl.pallas_call(
        functools.partial(_merge_kernel, alpha=alpha), grid=(m // tm,),
        in_specs=[row, row, row, row, gate(0), gate(1), gate(2), gate(3), row, wsp, wsp, wsp, wsp, wsp, vec, vec],
        out_specs=row, out_shape=jax.ShapeDtypeStruct((m, d), F32),
        compiler_params=_params("parallel"), name="merge")(
            oa, ob, oc, od, z, z, z, z, h, wa, wb, wc, wd, wo, ln_g, ln_b)


def _router_kernel(x_ref, w_ref, b_ref, idx_ref, wt_ref, rank_ref, cnt_ref, carry_ref, *, n_exp):
    i = pl.program_id(0)

    @pl.when(i == 0)
    def _():
        carry_ref[...] = jnp.zeros_like(carry_ref)

    tm = x_ref.shape[0]
    xs = _split3(x_ref[...])
    ws = _split3(w_ref[...])
    logits = (_dot(xs[0], ws[0]) + (_dot(xs[0], ws[1]) + _dot(xs[1], ws[0]))
              + (_dot(xs[1], ws[1]) + _dot(xs[0], ws[2]) + _dot(xs[2], ws[0]))) + b_ref[...]
    lane = lax.broadcasted_iota(jnp.int32, (tm, LANES), 1)
    lane_f = lane.astype(F32)
    work = jnp.where(lane < n_exp, logits, -jnp.inf)
    vals, hots = [], []
    idx_out = jnp.zeros((tm, LANES), F32)
    for kk in range(TOP_K):
        m = jnp.max(work, axis=-1, keepdims=True)
        first = jnp.min(jnp.where(work == m, lane_f, float(LANES)), axis=-1, keepdims=True)
        hot = lane_f == first
        work = jnp.where(hot, -jnp.inf, work)
        vals.append(m)
        hots.append(hot)
        idx_out = jnp.where(lane == kk, first, idx_out)
    es = [jnp.exp(v - vals[0]) for v in vals]
    den = es[0]
    for e in es[1:]:
        den = den + e
    wt = jnp.zeros((tm, LANES), F32)
    for kk in range(TOP_K):
        wt = jnp.where(lane == kk, es[kk] / den, wt)
    chosen = hots[0]
    for hot in hots[1:]:
        chosen = chosen | hot
    onehot = jnp.where(chosen, 1.0, 0.0)
    r = lax.broadcasted_iota(jnp.int32, (tm, tm), 0)
    c = lax.broadcasted_iota(jnp.int32, (tm, tm), 1)
    before = jnp.where(c < r, 1.0, 0.0).astype(BF16)
    seen = _dot(before, onehot.astype(BF16)) + carry_ref[...]
    rank = jnp.zeros((tm, LANES), F32)
    for kk in range(TOP_K):
        rk = jnp.sum(jnp.where(hots[kk], seen, 0.0), axis=-1, keepdims=True)
        rank = jnp.where(lane == kk, rk, rank)
    idx_ref[...] = idx_out.astype(jnp.int32)
    wt_ref[...] = wt
    rank_ref[...] = rank.astype(jnp.int32)
    carry_ref[...] = carry_ref[...] + jnp.sum(onehot, axis=0, keepdims=True)
    cnt_ref[...] = carry_ref[...].astype(jnp.int32)


def _router(x, w_pad, b_pad, n_exp):
    m, d = x.shape
    tm = _tile(m, 512, 8)
    row = pl.BlockSpec((tm, LANES), lambda i: (i, 0))
    one = pl.BlockSpec((1, LANES), lambda i: (0, 0))
    return pl.pallas_call(
        functools.partial(_router_kernel, n_exp=n_exp), grid=(m // tm,),
        in_specs=[pl.BlockSpec((tm, d), lambda i: (i, 0)), pl.BlockSpec((d, LANES), lambda i: (0, 0)), one],
        out_specs=[row, row, row, one],
        out_shape=[jax.ShapeDtypeStruct((m, LANES), jnp.int32), jax.ShapeDtypeStruct((m, LANES), F32),
                   jax.ShapeDtypeStruct((m, LANES), jnp.int32), jax.ShapeDtypeStruct((1, LANES), jnp.int32)],
        scratch_shapes=[pltpu.VMEM((1, LANES), F32)],
        compiler_params=_params("arbitrary"), name="router")(x, w_pad, b_pad)


def _dispatch_kernel(dest_ref, x_ref, zero_ref, xs_ref, sem, *, tt):
    del zero_ref

    def copy(n, kk):
        return pltpu.make_async_copy(x_ref.at[pl.ds(n, 1)], xs_ref.at[pl.ds(dest_ref[n * TOP_K + kk], 1)], sem)

    def start(n, carry):
        for kk in range(TOP_K):
            copy(n, kk).start()
        return carry

    def wait(n, carry):
        for kk in range(TOP_K):
            copy(n, kk).wait()
        return carry

    lax.fori_loop(0, tt, start, 0)
    lax.fori_loop(0, tt, wait, 0)


def _dispatch(x, dest_flat, rows):
    m, d = x.shape
    tt = _tile(m, 128, 8)
    return pl.pallas_call(
        functools.partial(_dispatch_kernel, tt=tt), grid=(m // tt,),
        in_specs=[pl.BlockSpec((tt * TOP_K,), lambda i: (i,), memory_space=pltpu.SMEM),
                  pl.BlockSpec((tt, d), lambda i: (i, 0)),
                  pl.BlockSpec(memory_space=pl.ANY)],
        out_specs=pl.BlockSpec(memory_space=pl.ANY),
        out_shape=jax.ShapeDtypeStruct((rows, d), F32),
        scratch_shapes=[pltpu.SemaphoreType.DMA(())],
        input_output_aliases={2: 0},
        compiler_params=_params("arbitrary"), name="moe_dispatch")(dest_flat, x, jnp.zeros((rows, d), F32))


def _expert_kernel(be_ref, nu_ref, xs_ref, wgu_ref, bgu_ref, wd_ref, bd_ref, y_ref, wgu_bf, wd_bf, *, d_ff):
    i = pl.program_id(0)
    prev = be_ref[jnp.maximum(i - 1, 0)]

    @pl.when((i == 0) | (be_ref[i] != prev))
    def _():
        wgu_bf[...] = wgu_ref[0].astype(BF16)
        wd_bf[...] = wd_ref[0].astype(BF16)

    @pl.when(i < nu_ref[0])
    def _():
        gu = _dot(xs_ref[...].astype(BF16), wgu_bf[...]) + bgu_ref[0]
        x_glu = jnp.minimum(gu[:, :d_ff], SWIGLU_LIMIT)
        x_lin = jnp.clip(gu[:, d_ff:], -SWIGLU_LIMIT, SWIGLU_LIMIT)
        act = x_glu * _sigmoid(SWIGLU_ALPHA * x_glu) * (x_lin + 1.0)
        y_ref[...] = _dot(act.astype(BF16), wd_bf[...]) + bd_ref[0]

    @pl.when(i >= nu_ref[0])
    def _():
        y_ref[...] = jnp.zeros_like(y_ref)


def _experts(xs, block_expert, n_used, w_gate_up, b_gate_up, w_down, b_down):
    rows, d = xs.shape
    n_exp, _, ff2 = w_gate_up.shape
    d_ff = ff2 // 2
    n_blocks = rows // MOE_ROWS
    blk = lambda i, be, nu: (jnp.minimum(i, nu[0] - 1), 0)
    grid_spec = pltpu.PrefetchScalarGridSpec(
        num_scalar_prefetch=2, grid=(n_blocks,),
        in_specs=[pl.BlockSpec((MOE_ROWS, d), blk),
                  pl.BlockSpec((1, d, ff2), lambda i, be, nu: (be[i], 0, 0)),
                  pl.BlockSpec((1, 1, ff2), lambda i, be, nu: (be[i], 0, 0)),
                  pl.BlockSpec((1, d_ff, d), lambda i, be, nu: (be[i], 0, 0)),
                  pl.BlockSpec((1, 1, d), lambda i, be, nu: (be[i], 0, 0))],
        out_specs=pl.BlockSpec((MOE_ROWS, d), lambda i, be, nu: (i, 0)),
        scratch_shapes=[pltpu.VMEM((d, ff2), BF16), pltpu.VMEM((d_ff, d), BF16)])
    return pl.pallas_call(
        functools.partial(_expert_kernel, d_ff=d_ff), grid_spec=grid_spec,
        out_shape=jax.ShapeDtypeStruct((rows, d), F32),
        compiler_params=_params("arbitrary"), name="moe_experts")(
            block_expert, n_used, xs, w_gate_up, b_gate_up.reshape(n_exp, 1, ff2),
            w_down, b_down.reshape(n_exp, 1, d))


def _combine_kernel(dest_ref, wt_ref, h_ref, lg_ref, lb_ref, y_ref, o_ref, rows, sem, *, tt, alpha):
    def copy(n, kk):
        return pltpu.make_async_copy(y_ref.at[pl.ds(dest_ref[n * TOP_K + kk], 1)], rows.at[kk, pl.ds(n, 1)], sem)

    def start(n, carry):
        for kk in range(TOP_K):
            copy(n, kk).start()
        return carry

    def wait(n, carry):
        for kk in range(TOP_K):
            copy(n, kk).wait()
        return carry

    lax.fori_loop(0, tt, start, 0)
    lax.fori_loop(0, tt, wait, 0)
    wt = wt_ref[...]
    moe = rows[0] * wt[:, 0:1]
    for kk in range(1, TOP_K):
        moe = moe + rows[kk] * wt[:, kk:kk + 1]
    o_ref[...] = _layer_norm(alpha * h_ref[...] + moe, lg_ref[...], lb_ref[...])


def _combine(y, dest_flat, wt, h, ln_g, ln_b, alpha):
    m, d = h.shape
    tt = _tile(m, 128, 8)
    vec = pl.BlockSpec((1, d), lambda i: (0, 0))
    return pl.pallas_call(
        functools.partial(_combine_kernel, tt=tt, alpha=alpha), grid=(m // tt,),
        in_specs=[pl.BlockSpec((tt * TOP_K,), lambda i: (i,), memory_space=pltpu.SMEM),
                  pl.BlockSpec((tt, LANES), lambda i: (i, 0)),
                  pl.BlockSpec((tt, d), lambda i: (i, 0)), vec, vec,
                  pl.BlockSpec(memory_space=pl.ANY)],
        out_specs=pl.BlockSpec((tt, d), lambda i: (i, 0)),
        out_shape=jax.ShapeDtypeStruct((m, d), F32),
        scratch_shapes=[pltpu.VMEM((TOP_K, tt, d), F32), pltpu.SemaphoreType.DMA(())],
        compiler_params=_params("arbitrary"), name="moe_combine")(dest_flat, wt, h, ln_g, ln_b, y)


def _moe(h, w, alpha):
    m, d = h.shape
    n_exp = w["w_router"].shape[1]
    idx, wt, rank, counts = _router(h, w["w_router_pad"], w["b_router_pad"], n_exp)
    counts = counts[0, :n_exp]
    padded = (counts + MOE_ROWS - 1) // MOE_ROWS * MOE_ROWS
    p_end = jnp.cumsum(padded)
    p_start = p_end - padded
    dest = (p_start[idx[:, :TOP_K]] + rank[:, :TOP_K]).reshape(-1)
    n_blocks = -(-(m * TOP_K) // MOE_ROWS) + n_exp
    block_expert = jnp.minimum(
        jnp.searchsorted(p_end, jnp.arange(n_blocks, dtype=jnp.int32) * MOE_ROWS, side="right"),
        n_exp - 1).astype(jnp.int32)
    n_used = (p_end[-1:] // MOE_ROWS).astype(jnp.int32)
    xs = _dispatch(h, dest, n_blocks * MOE_ROWS)
    y = _experts(xs, block_expert, n_used, w["w_gate_up"], w["b_gate_up"], w["w_down"], w["b_down"])
    return _combine(y, dest, wt, h, w["ln2_g"], w["ln2_b"], alpha)


def _layer(h, hb, w, dims, alpha, states):
    nb, t, d, gh, gdk, gdv, hh, hdk, hdv = dims
    z = _matmul(hb, w["w_in_main"])
    la = _gla_decay(hb, w["w_lra"], w["w_gla_lr"], w["b_gla_lr"])
    col_conv = (2 * gh * gdk + 2 * gh * gdv) // d
    col_hgrn = col_conv + 5
    col_gate = col_hgrn + 4
    if states is None:
        oa, s_gla = _gla_prompt(z, la, w["gla_norm_g"], nb, t, gh, gdk, gdv)
        od, s_hgrn = _hgrn_prompt(z, w["hgrn_lb"], w["hgrn_norm_g"], nb, t, hh, hdk, hdv, col_hgrn)
        ob, oc, c_s, c_c = _conv_prompt(z, w["sconv_w"], w["conf_conv_w"], w["conf_conv_b"],
                                        w["conf_ln_g"], w["conf_ln_b"], nb, t, d, col_conv)
    else:
        st_gla, st_hgrn, cache_s, cache_c = states
        oa, s_gla = _gla_step(z, la, w["gla_norm_g"], st_gla, gh, gdk, gdv)
        od, s_hgrn = _hgrn_step(z, w["hgrn_lb"], w["hgrn_norm_g"], st_hgrn, hh, hdk, hdv, col_hgrn)
        ob, oc, u, uc = _conv_step(z, cache_s, cache_c, w["sconv_w"], w["conf_conv_w"], w["conf_conv_b"],
                                   w["conf_ln_g"], w["conf_ln_b"], d, col_conv)
        c_s = jnp.concatenate([cache_s[:, 1:], u[:, None]], axis=1)
        c_c = jnp.concatenate([cache_c[:, 1:], uc[:, None]], axis=1)
    h1 = _merge(oa, ob, oc, od, z, h, w["w_br_a"], w["w_br_b"], w["w_br_c"], w["w_br_d"], w["w_o"],
                w["ln1_g"], w["ln1_b"], alpha, col_gate)
    h2 = _moe(h1, w, alpha)
    return h2, (s_gla, s_hgrn, c_s, c_c)


def _trunk(x, states, ln_in_g, ln_in_b, layers, dims, alpha):
    nb, t, d = dims[:3]
    h, hb = _ln_call(x.reshape(nb * t, d), ln_in_g, ln_in_b)
    new = ([], [], [], [])
    for l, w in enumerate(layers):
        st = None if states is None else tuple(s[l] for s in states)
        h, out = _layer(h, hb, w, dims, alpha, st)
        hb = h.astype(BF16)
        for acc, s in zip(new, out):
            acc.append(s)
    return h.reshape(nb, t, d), tuple(jnp.stack(a) for a in new)


def kernel(x_prompt, x_sample, state_gla, state_hgrn, cache_sconv, cache_conformer, ln_in_g, ln_in_b, w_in, w_gla_lr, b_gla_lr, gla_norm_g, w_br_a, sconv_w, w_br_b, conf_conv_w, conf_conv_b, conf_ln_g, conf_ln_b, w_br_c, hgrn_lb_logits, hgrn_norm_g, w_br_d, w_o, ln1_g, ln1_b, w_router, b_router, w_gate_up, b_gate_up, w_down, b_down, ln2_g, ln2_b):
    depth, d, _ = w_in.shape
    _, _, gh, gdk, gdv = state_gla.shape
    _, _, hh, hdk, hdv = state_hgrn.shape
    rank = w_gla_lr.shape[1]
    n_exp = w_router.shape[2]
    alpha = (2 * depth) ** 0.25
    lra0 = 2 * gh * gdk + 2 * gh * gdv

    lb_p = jax.nn.softmax(hgrn_lb_logits.astype(F32), axis=0)
    hgrn_lb = jnp.cumsum(lb_p, axis=0) - lb_p[:1]
    row = lambda a: a.reshape(1, -1)
    layers = []
    for l in range(depth):
        w_lra = jnp.zeros((d, LANES), F32).at[:, :rank].set(w_in[l, :, lra0:lra0 + rank])
        w2 = jnp.zeros((LANES, gh * gdk), F32).at[:rank].set(w_gla_lr[l])
        layers.append({
            "w_in_main": jnp.concatenate([w_in[l, :, :lra0], w_in[l, :, lra0 + rank:]], axis=1).astype(BF16),
            "w_lra": w_lra.astype(BF16), "w_gla_lr": w2.astype(BF16), "b_gla_lr": row(b_gla_lr[l]),
            "gla_norm_g": row(gla_norm_g[l]), "hgrn_norm_g": row(hgrn_norm_g[l]), "hgrn_lb": row(hgrn_lb[l]),
            "sconv_w": sconv_w[l], "conf_conv_w": conf_conv_w[l], "conf_conv_b": row(conf_conv_b[l]),
            "conf_ln_g": row(conf_ln_g[l]), "conf_ln_b": row(conf_ln_b[l]),
            "w_br_a": w_br_a[l].astype(BF16), "w_br_b": w_br_b[l].astype(BF16),
            "w_br_c": w_br_c[l].astype(BF16), "w_br_d": w_br_d[l].astype(BF16), "w_o": w_o[l].astype(BF16),
            "ln1_g": row(ln1_g[l]), "ln1_b": row(ln1_b[l]), "ln2_g": row(ln2_g[l]), "ln2_b": row(ln2_b[l]),
            "w_router": w_router[l],
            "w_router_pad": jnp.zeros((d, LANES), F32).at[:, :n_exp].set(w_router[l]),
            "b_router_pad": jnp.zeros((1, LANES), F32).at[0, :n_exp].set(b_router[l]),
            "w_gate_up": w_gate_up[l], "b_gate_up": b_gate_up[l], "w_down": w_down[l], "b_down": b_down[l],
        })

    nbp, tp, _ = x_prompt.shape
    nbs, ts, _ = x_sample.shape
    assert ts == 1, "the sample group advances its states by exactly one token"
    y_p, st_p = _trunk(x_prompt, None, ln_in_g, ln_in_b, layers, (nbp, tp, d, gh, gdk, gdv, hh, hdk, hdv), alpha)
    y_s, st_s = _trunk(x_sample, (state_gla, state_hgrn, cache_sconv, cache_conformer), ln_in_g, ln_in_b,
                       layers, (nbs, ts, d, gh, gdk, gdv, hh, hdk, hdv), alpha)
    return (y_p, y_s) + st_p + st_s
```

```python
import functools

import jax
import jax.numpy as jnp
from jax import lax
from jax.experimental import pallas as pl
from jax.experimental.pallas import tpu as pltpu

F32 = jnp.float32
BF16 = jnp.bfloat16

TOP_K = 4
CHUNK = 64
SUB = 16
GLA_GATE_NORM = 16.0
F_MIN = 1e-20
LN_EPS = 1e-5
RMS_EPS = 1e-6
SWIGLU_ALPHA = 1.702
SWIGLU_LIMIT = 7.0
LOG2_E = 1.4426950408889634
MOE_ROWS = 512
LANES = 128
SUBLANES = 8
CONV_ROWS = 32
CONF_HIST = 32
SCONV_HIST = 8
V7X_VMEM_LIMIT = 56 * 1024 * 1024


def _params(*sem):
    return pltpu.CompilerParams(dimension_semantics=sem, vmem_limit_bytes=V7X_VMEM_LIMIT)


def _tile(n, pref, mult=8):
    if n <= pref:
        return n
    t = pref - pref % mult
    while t >= mult:
        if n % t == 0:
            return t
        t -= mult
    raise ValueError(f"no tile for {n}")


def _sigmoid(x):
    return 1.0 / (1.0 + jnp.exp(-x))


def _silu(x):
    return x * _sigmoid(x)


def _log_sigmoid(x):
    return jnp.minimum(x, 0.0) - jnp.log(1.0 + jnp.exp(-jnp.abs(x)))


def _layer_norm(x, g, b):
    mu = jnp.mean(x, axis=-1, keepdims=True)
    xc = x - mu
    var = jnp.mean(xc * xc, axis=-1, keepdims=True)
    return xc * lax.rsqrt(var + LN_EPS) * g + b


def _dot(a, b):
    return jnp.dot(a, b, preferred_element_type=F32)


def _dot_nt(a, b):
    return lax.dot_general(a, b, (((1,), (1,)), ((), ())), preferred_element_type=F32)


def _dot_tn(a, b):
    return lax.dot_general(a, b, (((0,), (0,)), ((), ())), preferred_element_type=F32)


def _split3(x):
    hi = x.astype(BF16)
    r = x - hi.astype(F32)
    mid = r.astype(BF16)
    lo = (r - mid.astype(F32)).astype(BF16)
    return hi, mid, lo


def _ln_kernel(x_ref, g_ref, b_ref, o_ref, ob_ref):
    y = _layer_norm(x_ref[...], g_ref[...], b_ref[...])
    o_ref[...] = y
    ob_ref[...] = y.astype(BF16)


def _ln_call(x, g, b):
    m, d = x.shape
    tm = _tile(m, 512, 16)
    row = pl.BlockSpec((tm, d), lambda i: (i, 0))
    vec = pl.BlockSpec((1, d), lambda i: (0, 0))
    return pl.pallas_call(
        _ln_kernel, grid=(m // tm,), in_specs=[row, vec, vec], out_specs=[row, row],
        out_shape=[jax.ShapeDtypeStruct((m, d), F32), jax.ShapeDtypeStruct((m, d), BF16)],
        compiler_params=_params("parallel"), name="ln_in")(x, g.reshape(1, d), b.reshape(1, d))


def _mm_kernel(x_ref, w_ref, o_ref):
    o_ref[...] = _dot(x_ref[...], w_ref[...])


def _matmul(x, w):
    m, k = x.shape
    n = w.shape[1]
    tm = _tile(m, 1024, 16)
    tn = _tile(n, 2048, LANES)
    return pl.pallas_call(
        _mm_kernel, grid=(n // tn, m // tm),
        in_specs=[pl.BlockSpec((tm, k), lambda j, i: (i, 0)), pl.BlockSpec((k, tn), lambda j, i: (0, j))],
        out_specs=pl.BlockSpec((tm, tn), lambda j, i: (i, j)),
        out_shape=jax.ShapeDtypeStruct((m, n), F32),
        compiler_params=_params("parallel", "parallel"), name="in_proj")(x, w)


def _gla_decay_kernel(h_ref, w1_ref, w2_ref, b2_ref, o_ref):
    lra = _dot(h_ref[...], w1_ref[...])
    x = _dot(lra.astype(BF16), w2_ref[...]) + b2_ref[...]
    o_ref[...] = _log_sigmoid(x) * (1.0 / GLA_GATE_NORM)


def _gla_decay(hb, w1, w2, b2):
    m, d = hb.shape
    n = w2.shape[1]
    tm = _tile(m, 512, 16)
    return pl.pallas_call(
        _gla_decay_kernel, grid=(m // tm,),
        in_specs=[pl.BlockSpec((tm, d), lambda i: (i, 0)), pl.BlockSpec(w1.shape, lambda i: (0, 0)),
                  pl.BlockSpec(w2.shape, lambda i: (0, 0)), pl.BlockSpec((1, n), lambda i: (0, 0))],
        out_specs=pl.BlockSpec((tm, n), lambda i: (i, 0)),
        out_shape=jax.ShapeDtypeStruct((m, n), F32),
        compiler_params=_params("parallel"), name="gla_decay")(hb, w1, w2, b2)


def _cumsum_rows(x):
    c = x.shape[0]
    r = lax.broadcasted_iota(jnp.int32, (c, c), 0)
    s = lax.broadcasted_iota(jnp.int32, (c, c), 1)
    tri = jnp.where(r >= s, 1.0, 0.0).astype(BF16)
    hi, mid, lo = _split3(x)
    return _dot(tri, hi) + _dot(tri, mid) + _dot(tri, lo)


def _intra_scores(q, k, b2):
    c, dk = q.shape
    half = SUB // 2
    lane = lax.broadcasted_iota(jnp.int32, (half, c), 1)
    row = lax.broadcasted_iota(jnp.int32, (half, c), 0)
    blocks = []
    for i in range(c // SUB):
        lo = i * SUB
        qi, ki, bi = q[lo:lo + SUB], k[lo:lo + SUB], b2[lo:lo + SUB]
        if i > 0:
            ref = b2[lo - 1:lo]
            qs = qi * jnp.exp2(bi - ref)
            ks = jnp.concatenate([k[:lo] * jnp.exp2(ref - b2[:lo]), jnp.zeros((c - lo, dk), F32)], axis=0)
            a = _dot_nt(qs.astype(BF16), ks.astype(BF16))
            top, bot = a[:half], a[half:]
        else:
            top = bot = jnp.zeros((half, c), F32)
        for s in range(SUB):
            ks_row, bs_row = ki[s:s + 1], bi[s:s + 1]
            if s < half:
                p = qi[:half] * ks_row * jnp.exp2(bi[:half] - bs_row)
                top = jnp.where((lane == lo + s) & (row >= s), jnp.sum(p, axis=-1, keepdims=True), top)
            p = qi[half:] * ks_row * jnp.exp2(bi[half:] - bs_row)
            keep = (lane == lo + s) if s < half else (lane == lo + s) & (row >= s - half)
            bot = jnp.where(keep, jnp.sum(p, axis=-1, keepdims=True), bot)
        blocks += [top, bot]
    return jnp.concatenate(blocks, axis=0)


def _chunk_step(q, k, v, b, st):
    c = q.shape[0]
    b2 = b * LOG2_E
    b_end = b2[c - 1:c]
    o = _dot_nt((q * jnp.exp2(b2)).astype(BF16), st.astype(BF16))
    a = _intra_scores(q, k, b2)
    vb = v.astype(BF16)
    o = o + _dot(a.astype(BF16), vb)
    kb = (k * jnp.exp2(b_end - b2)).astype(BF16)
    st_new = st * jnp.exp2(b_end) + _dot_tn(vb, kb)
    return o, st_new


def _gated_rms(o, gate, g):
    o = o * lax.rsqrt(jnp.mean(o * o, axis=-1, keepdims=True) + RMS_EPS) * g
    return o * _silu(gate)


def _gla_kernel(q_ref, k_ref, v_ref, g_ref, la_ref, ng_ref, o_ref, so_ref, st_ref, *, heads, dk, dv):
    j = pl.program_id(1)

    @pl.when(j == 0)
    def _():
        st_ref[...] = jnp.zeros_like(st_ref)

    b_all = _cumsum_rows(la_ref[...])
    scale = dk ** -0.5
    for h in range(heads):
        ks = slice(h * dk, (h + 1) * dk)
        vs = slice(h * dv, (h + 1) * dv)
        o, st = _chunk_step(q_ref[:, ks] * scale, k_ref[:, ks], v_ref[:, vs], b_all[:, ks], st_ref[h])
        st_ref[h] = st
        o_ref[:, vs] = _gated_rms(o, g_ref[:, vs], ng_ref[...]).astype(BF16)

    @pl.when(j == pl.num_programs(1) - 1)
    def _():
        for h in range(heads):
            so_ref[0, h] = st_ref[h].T


def _gla_prompt(z, la, norm_g, nb, t, heads, dk, dv):
    c = min(CHUNK, t)
    n = t // c
    wk, wv = heads * dk, heads * dv
    row = lambda col: (lambda bi, j: (bi * n + j, col))
    kern = functools.partial(_gla_kernel, heads=heads, dk=dk, dv=dv)
    return pl.pallas_call(
        kern, grid=(nb, n),
        in_specs=[pl.BlockSpec((c, wk), row(0)), pl.BlockSpec((c, wk), row(1)),
                  pl.BlockSpec((c, wv), row(1)), pl.BlockSpec((c, wv), row(2)),
                  pl.BlockSpec((c, wk), row(0)), pl.BlockSpec((1, dv), lambda bi, j: (0, 0))],
        out_specs=[pl.BlockSpec((c, wv), row(0)),
                   pl.BlockSpec((1, heads, dk, dv), lambda bi, j: (bi, 0, 0, 0))],
        out_shape=[jax.ShapeDtypeStruct((nb * t, wv), BF16),
                   jax.ShapeDtypeStruct((nb, heads, dk, dv), F32)],
        scratch_shapes=[pltpu.VMEM((heads, dv, dk), F32)],
        compiler_params=_params("parallel", "arbitrary"), name="gla_prompt")(z, z, z, z, la, norm_g)


def _hgrn_inputs(qd, fd, lb):
    f = lb + (1.0 - lb) * _sigmoid(fd)
    return _silu(qd), 1.0 - f, jnp.log(jnp.maximum(f, F_MIN))


def _hgrn_kernel(q_ref, f_ref, v_ref, g_ref, lb_ref, ng_ref, o_ref, so_ref, st_ref, *, heads, dk, dv):
    j = pl.program_id(1)

    @pl.when(j == 0)
    def _():
        st_ref[...] = jnp.zeros_like(st_ref)

    q_all, k_all, lf = _hgrn_inputs(q_ref[...], f_ref[...], lb_ref[...])
    b_all = _cumsum_rows(lf)
    for h in range(heads):
        ks = slice(h * dk, (h + 1) * dk)
        vs = slice(h * dv, (h + 1) * dv)
        o, st = _chunk_step(q_all[:, ks], k_all[:, ks], v_ref[:, vs], b_all[:, ks], st_ref[h])
        st_ref[h] = st
        o_ref[:, vs] = _gated_rms(o, g_ref[:, vs], ng_ref[...]).astype(BF16)

    @pl.when(j == pl.num_programs(1) - 1)
    def _():
        for h in range(heads):
            so_ref[0, h] = st_ref[h].T


def _hgrn_prompt(z, lb, norm_g, nb, t, heads, dk, dv, col0):
    c = min(CHUNK, t)
    n = t // c
    w = heads * dk
    row = lambda col: (lambda bi, j: (bi * n + j, col0 + col))
    kern = functools.partial(_hgrn_kernel, heads=heads, dk=dk, dv=dv)
    return pl.pallas_call(
        kern, grid=(nb, n),
        in_specs=[pl.BlockSpec((c, w), row(0)), pl.BlockSpec((c, w), row(1)),
                  pl.BlockSpec((c, w), row(2)), pl.BlockSpec((c, w), row(3)),
                  pl.BlockSpec((1, w), lambda bi, j: (0, 0)), pl.BlockSpec((1, dv), lambda bi, j: (0, 0))],
        out_specs=[pl.BlockSpec((c, w), lambda bi, j: (bi * n + j, 0)),
                   pl.BlockSpec((1, heads, dk, dv), lambda bi, j: (bi, 0, 0, 0))],
        out_shape=[jax.ShapeDtypeStruct((nb * t, w), BF16),
                   jax.ShapeDtypeStruct((nb, heads, dk, dv), F32)],
        scratch_shapes=[pltpu.VMEM((heads, dv, dk), F32)],
        compiler_params=_params("parallel", "arbitrary"), name="hgrn_prompt")(z, z, z, z, lb, norm_g)


def _columns(x, n):
    w = x.shape[1]
    pad = jnp.concatenate([x, jnp.zeros((w - n, w), F32)], axis=0) if n < w else x
    return pad.T


def _step_heads(q_all, k_all, a_all, v_ref, g_ref, ng_ref, s_ref, o_ref, so_ref, *, heads, dk, dv, tb):
    for h in range(heads):
        ks = slice(h * dk, (h + 1) * dk)
        vs = slice(h * dv, (h + 1) * dv)
        qc, kc, ac = _columns(q_all[:, ks], tb), _columns(k_all[:, ks], tb), _columns(a_all[:, ks], tb)
        v = v_ref[:, vs]
        row = lax.broadcasted_iota(jnp.int32, (tb, dv), 0)
        o = jnp.zeros((tb, dv), F32)
        for n in range(tb):
            s_new = ac[:, n:n + 1] * s_ref[0, n, h] + kc[:, n:n + 1] * v[n:n + 1]
            so_ref[0, n, h] = s_new
            o = jnp.where(row == n, jnp.sum(qc[:, n:n + 1] * s_new, axis=0, keepdims=True), o)
        o_ref[:, vs] = _gated_rms(o, g_ref[:, vs], ng_ref[...]).astype(BF16)


def _gla_step_kernel(q_ref, k_ref, v_ref, g_ref, la_ref, ng_ref, s_ref, prev_ref, o_ref, so_ref,
                     *, heads, dk, dv, tb):
    del prev_ref
    _step_heads(q_ref[...] * dk ** -0.5, k_ref[...], jnp.exp(la_ref[...]), v_ref, g_ref, ng_ref, s_ref,
                o_ref, so_ref, heads=heads, dk=dk, dv=dv, tb=tb)


def _hgrn_step_kernel(q_ref, f_ref, v_ref, g_ref, lb_ref, ng_ref, s_ref, prev_ref, o_ref, so_ref,
                      *, heads, dk, dv, tb):
    del prev_ref
    q_all, k_all, lf = _hgrn_inputs(q_ref[...], f_ref[...], lb_ref[...])
    _step_heads(q_all, k_all, jnp.exp(lf), v_ref, g_ref, ng_ref, s_ref, o_ref, so_ref,
                heads=heads, dk=dk, dv=dv, tb=tb)


def _gla_step(z, la, norm_g, states, new_states, layer):
    m = z.shape[0]
    _, _, heads, dk, dv = states.shape
    tb = 8
    wk, wv = heads * dk, heads * dv
    row = lambda col: (lambda i: (i, col))
    st = pl.BlockSpec((1, tb, heads, dk, dv), lambda i: (layer, i, 0, 0, 0))
    kern = functools.partial(_gla_step_kernel, heads=heads, dk=dk, dv=dv, tb=tb)
    return pl.pallas_call(
        kern, grid=(m // tb,),
        in_specs=[pl.BlockSpec((tb, wk), row(0)), pl.BlockSpec((tb, wk), row(1)),
                  pl.BlockSpec((tb, wv), row(1)), pl.BlockSpec((tb, wv), row(2)),
                  pl.BlockSpec((tb, wk), row(0)), pl.BlockSpec((1, dv), lambda i: (0, 0)), st,
                  pl.BlockSpec(memory_space=pl.ANY)],
        out_specs=[pl.BlockSpec((tb, wv), row(0)), st],
        out_shape=[jax.ShapeDtypeStruct((m, wv), BF16), jax.ShapeDtypeStruct(states.shape, F32)],
        input_output_aliases={7: 1},
        compiler_params=_params("parallel"), name="gla_step")(z, z, z, z, la, norm_g, states, new_states)


def _hgrn_step(z, lb, norm_g, states, new_states, layer, col0):
    m = z.shape[0]
    _, _, heads, dk, dv = states.shape
    tb = 8
    w = heads * dk
    row = lambda col: (lambda i: (i, col0 + col))
    st = pl.BlockSpec((1, tb, heads, dk, dv), lambda i: (layer, i, 0, 0, 0))
    kern = functools.partial(_hgrn_step_kernel, heads=heads, dk=dk, dv=dv, tb=tb)
    return pl.pallas_call(
        kern, grid=(m // tb,),
        in_specs=[pl.BlockSpec((tb, w), row(0)), pl.BlockSpec((tb, w), row(1)),
                  pl.BlockSpec((tb, w), row(2)), pl.BlockSpec((tb, w), row(3)),
                  pl.BlockSpec((1, w), lambda i: (0, 0)), pl.BlockSpec((1, dv), lambda i: (0, 0)), st,
                  pl.BlockSpec(memory_space=pl.ANY)],
        out_specs=[pl.BlockSpec((tb, w), lambda i: (i, 0)), st],
        out_shape=[jax.ShapeDtypeStruct((m, w), BF16), jax.ShapeDtypeStruct(states.shape, F32)],
        input_output_aliases={7: 1},
        compiler_params=_params("parallel"), name="hgrn_step")(z, z, z, z, lb, norm_g, states, new_states)


def _taps(win, w_ref, lanes, first, n_taps):
    n = win.shape[0]
    acc = None
    for res in range(SUBLANES):
        offs = [o for o in range(first, first + n_taps) if o % SUBLANES == res]
        if not offs:
            continue
        sh = win if res == 0 else pltpu.roll(win, n - res, axis=0)
        for o in offs:
            base = o - res
            term = w_ref[o - first:o - first + 1, lanes] * sh[base:base + CONV_ROWS]
            acc = term if acc is None else acc + term
    return acc


def _conv_kernel(gb_ref, gc_ref, hb_ref, ga_ref, gs_ref, sw_ref, cw_ref, cb_ref, lg_ref, lb_ref,
                 ob_ref, oc_ref, so_ref, co_ref, ubuf, cbuf, *, tt, sw, cw):
    t = pl.program_id(1)

    @pl.when(t == 0)
    def _():
        ubuf[0:SCONV_HIST] = jnp.zeros((SCONV_HIST, ubuf.shape[1]), F32)
        cbuf[0:CONF_HIST] = jnp.zeros((CONF_HIST, cbuf.shape[1]), F32)

    ubuf[SCONV_HIST:SCONV_HIST + tt] = gc_ref[...] * hb_ref[...]
    cbuf[CONF_HIST:CONF_HIST + tt] = ga_ref[...] * _sigmoid(gs_ref[...])
    s0 = SCONV_HIST - (sw - 1)
    c0 = CONF_HIST - (cw - 1)

    def strip(r, carry):
        r0 = pl.multiple_of(r * CONV_ROWS, CONV_ROWS)
        yb, yc = [], []
        for lb in range(ubuf.shape[1] // LANES):
            ls = slice(lb * LANES, (lb + 1) * LANES)
            yb.append(_taps(ubuf[pl.ds(r0, CONV_ROWS + SCONV_HIST), ls], sw_ref, ls, s0, sw))
            yc.append(_taps(cbuf[pl.ds(r0, CONV_ROWS + CONF_HIST), ls], cw_ref, ls, c0, cw))
        acc = jnp.concatenate(yb, axis=1)
        ob_ref[pl.ds(r0, CONV_ROWS)] = (gb_ref[pl.ds(r0, CONV_ROWS)] * acc).astype(BF16)
        acc = jnp.concatenate(yc, axis=1)
        y = _layer_norm(acc + cb_ref[...], lg_ref[...], lb_ref[...])
        oc_ref[pl.ds(r0, CONV_ROWS)] = _silu(y).astype(BF16)
        return carry

    lax.fori_loop(0, tt // CONV_ROWS, strip, 0)

    @pl.when(t == pl.num_programs(1) - 1)
    def _():
        so_ref[0] = ubuf[SCONV_HIST + tt - (sw - 1):SCONV_HIST + tt]
        co_ref[0] = cbuf[CONF_HIST + tt - (cw - 1):CONF_HIST + tt]

    ubuf[0:SCONV_HIST] = ubuf[tt:tt + SCONV_HIST]
    cbuf[0:CONF_HIST] = cbuf[tt:tt + CONF_HIST]


def _conv_prompt(z, sconv_w, conf_w, conf_b, ln_g, ln_b, nb, t, d, col0):
    sw, cw = sconv_w.shape[0], conf_w.shape[0]
    tt = _tile(t, 256, CONF_HIST)
    n = t // tt
    row = lambda col: (lambda bi, j: (bi * n + j, col0 + col))
    vec = lambda r: pl.BlockSpec((r, d), lambda bi, j: (0, 0))
    kern = functools.partial(_conv_kernel, tt=tt, sw=sw, cw=cw)
    blk = lambda col: pl.BlockSpec((tt, d), row(col))
    return pl.pallas_call(
        kern, grid=(nb, n),
        in_specs=[blk(0), blk(1), blk(2), blk(3), blk(4), vec(sw), vec(cw), vec(1), vec(1), vec(1)],
        out_specs=[pl.BlockSpec((tt, d), lambda bi, j: (bi * n + j, 0)),
                   pl.BlockSpec((tt, d), lambda bi, j: (bi * n + j, 0)),
                   pl.BlockSpec((1, sw - 1, d), lambda bi, j: (bi, 0, 0)),
                   pl.BlockSpec((1, cw - 1, d), lambda bi, j: (bi, 0, 0))],
        out_shape=[jax.ShapeDtypeStruct((nb * t, d), BF16), jax.ShapeDtypeStruct((nb * t, d), BF16),
                   jax.ShapeDtypeStruct((nb, sw - 1, d), F32), jax.ShapeDtypeStruct((nb, cw - 1, d), F32)],
        scratch_shapes=[pltpu.VMEM((SCONV_HIST + tt, d), F32), pltpu.VMEM((CONF_HIST + tt, d), F32)],
        compiler_params=_params("parallel", "arbitrary"), name="conv_prompt")(
            z, z, z, z, z, sconv_w, conf_w, conf_b, ln_g, ln_b)


def _conv_step_kernel(gb_ref, gc_ref, hb_ref, ga_ref, gs_ref, sc_ref, cc_ref, sw_ref, cw_ref, cb_ref,
                      lg_ref, lb_ref, ob_ref, oc_ref, u_ref, uc_ref, *, tb, sw, cw):
    u = gc_ref[...] * hb_ref[...]
    uc = ga_ref[...] * _sigmoid(gs_ref[...])
    u_ref[...] = u
    uc_ref[...] = uc
    row = lax.broadcasted_iota(jnp.int32, u.shape, 0)
    ys = jnp.zeros_like(u)
    yc = jnp.zeros_like(u)
    for n in range(tb):
        ys = jnp.where(row == n, jnp.sum(sc_ref[n] * sw_ref[0:sw - 1], axis=0, keepdims=True), ys)
        yc = jnp.where(row == n, jnp.sum(cc_ref[n] * cw_ref[0:cw - 1], axis=0, keepdims=True), yc)
    yb = ys + sw_ref[sw - 1:sw] * u
    ob_ref[...] = (gb_ref[...] * yb).astype(BF16)
    y = yc + cw_ref[cw - 1:cw] * uc + cb_ref[...]
    oc_ref[...] = _silu(_layer_norm(y, lg_ref[...], lb_ref[...])).astype(BF16)


def _conv_step(z, cache_s, cache_c, sconv_w, conf_w, conf_b, ln_g, ln_b, d, col0):
    m = z.shape[0]
    sw, cw = sconv_w.shape[0], conf_w.shape[0]
    tb = 16
    blk = lambda col: pl.BlockSpec((tb, d), lambda i: (i, col0 + col))
    vec = lambda r: pl.BlockSpec((r, d), lambda i: (0, 0))
    out = pl.BlockSpec((tb, d), lambda i: (i, 0))
    kern = functools.partial(_conv_step_kernel, tb=tb, sw=sw, cw=cw)
    return pl.pallas_call(
        kern, grid=(m // tb,),
        in_specs=[blk(0), blk(1), blk(2), blk(3), blk(4),
                  pl.BlockSpec((tb, sw - 1, d), lambda i: (i, 0, 0)),
                  pl.BlockSpec((tb, cw - 1, d), lambda i: (i, 0, 0)),
                  vec(sw), vec(cw), vec(1), vec(1), vec(1)],
        out_specs=[out, out, out, out],
        out_shape=[jax.ShapeDtypeStruct((m, d), BF16), jax.ShapeDtypeStruct((m, d), BF16),
                   jax.ShapeDtypeStruct((m, d), F32), jax.ShapeDtypeStruct((m, d), F32)],
        compiler_params=_params("parallel"), name="conv_step")(
            z, z, z, z, z, cache_s, cache_c, sconv_w, conf_w, conf_b, ln_g, ln_b)


def _merge_kernel(oa_ref, ob_ref, oc_ref, od_ref, g0_ref, g1_ref, g2_ref, g3_ref, h_ref,
                  wa_ref, wb_ref, wc_ref, wd_ref, wo_ref, lg_ref, lb_ref, o_ref, *, alpha):
    merged = _sigmoid(g0_ref[...]) * _dot(oa_ref[...], wa_ref[...])
    merged = merged + _sigmoid(g1_ref[...]) * _dot(ob_ref[...], wb_ref[...])
    merged = merged + _sigmoid(g2_ref[...]) * _dot(oc_ref[...], wc_ref[...])
    merged = merged + _sigmoid(g3_ref[...]) * _dot(od_ref[...], wd_ref[...])
    out = _dot(merged.astype(BF16), wo_ref[...])
    o_ref[...] = _layer_norm(alpha * h_ref[...] + out, lg_ref[...], lb_ref[...])


def _merge(oa, ob, oc, od, z, h, wa, wb, wc, wd, wo, ln_g, ln_b, alpha, gate_col0):
    m, d = h.shape
    tm = _tile(m, 256, 16)
    row = pl.BlockSpec((tm, d), lambda i: (i, 0))
    gate = lambda c: pl.BlockSpec((tm, d), lambda i: (i, gate_col0 + c))
    wsp = pl.BlockSpec((d, d), lambda i: (0, 0))
    vec = pl.BlockSpec((1, d), lambda i: (0, 0))
    return pl.pallas_call(
        functools.partial(_merge_kernel, alpha=alpha), grid=(m // tm,),
        in_specs=[row, row, row, row, gate(0), gate(1), gate(2), gate(3), row, wsp, wsp, wsp, wsp, wsp, vec, vec],
        out_specs=row, out_shape=jax.ShapeDtypeStruct((m, d), F32),
        compiler_params=_params("parallel"), name="merge")(
            oa, ob, oc, od, z, z, z, z, h, wa, wb, wc, wd, wo, ln_g, ln_b)


def _router_kernel(x_ref, w_ref, b_ref, idx_ref, wt_ref, rank_ref, cnt_ref, carry_ref, *, n_exp):
    i = pl.program_id(0)

    @pl.when(i == 0)
    def _():
        carry_ref[...] = jnp.zeros_like(carry_ref)

    tm = x_ref.shape[0]
    xs = _split3(x_ref[...])
    ws = _split3(w_ref[...])
    logits = (_dot(xs[0], ws[0]) + (_dot(xs[0], ws[1]) + _dot(xs[1], ws[0]))
              + (_dot(xs[1], ws[1]) + _dot(xs[0], ws[2]) + _dot(xs[2], ws[0]))) + b_ref[...]
    lane = lax.broadcasted_iota(jnp.int32, (tm, LANES), 1)
    lane_f = lane.astype(F32)
    work = jnp.where(lane < n_exp, logits, -jnp.inf)
    vals, hots = [], []
    idx_out = jnp.zeros((tm, LANES), F32)
    for kk in range(TOP_K):
        m = jnp.max(work, axis=-1, keepdims=True)
        first = jnp.min(jnp.where(work == m, lane_f, float(LANES)), axis=-1, keepdims=True)
        hot = lane_f == first
        work = jnp.where(hot, -jnp.inf, work)
        vals.append(m)
        hots.append(hot)
        idx_out = jnp.where(lane == kk, first, idx_out)
    es = [jnp.exp(v - vals[0]) for v in vals]
    den = es[0]
    for e in es[1:]:
        den = den + e
    wt = jnp.zeros((tm, LANES), F32)
    for kk in range(TOP_K):
        wt = jnp.where(lane == kk, es[kk] / den, wt)
    chosen = hots[0]
    for hot in hots[1:]:
        chosen = chosen | hot
    onehot = jnp.where(chosen, 1.0, 0.0)
    r = lax.broadcasted_iota(jnp.int32, (tm, tm), 0)
    c = lax.broadcasted_iota(jnp.int32, (tm, tm), 1)
    before = jnp.where(c < r, 1.0, 0.0).astype(BF16)
    seen = _dot(before, onehot.astype(BF16)) + carry_ref[...]
    rank = jnp.zeros((tm, LANES), F32)
    for kk in range(TOP_K):
        rk = jnp.sum(jnp.where(hots[kk], seen, 0.0), axis=-1, keepdims=True)
        rank = jnp.where(lane == kk, rk, rank)
    idx_ref[...] = idx_out.astype(jnp.int32)
    wt_ref[...] = wt
    rank_ref[...] = rank.astype(jnp.int32)
    carry_ref[...] = carry_ref[...] + jnp.sum(onehot, axis=0, keepdims=True)
    cnt_ref[...] = carry_ref[...].astype(jnp.int32)


def _router(x, w_pad, b_pad, n_exp):
    m, d = x.shape
    tm = _tile(m, 512, 8)
    row = pl.BlockSpec((tm, LANES), lambda i: (i, 0))
    one = pl.BlockSpec((1, LANES), lambda i: (0, 0))
    return pl.pallas_call(
        functools.partial(_router_kernel, n_exp=n_exp), grid=(m // tm,),
        in_specs=[pl.BlockSpec((tm, d), lambda i: (i, 0)), pl.BlockSpec((d, LANES), lambda i: (0, 0)), one],
        out_specs=[row, row, row, one],
        out_shape=[jax.ShapeDtypeStruct((m, LANES), jnp.int32), jax.ShapeDtypeStruct((m, LANES), F32),
                   jax.ShapeDtypeStruct((m, LANES), jnp.int32), jax.ShapeDtypeStruct((1, LANES), jnp.int32)],
        scratch_shapes=[pltpu.VMEM((1, LANES), F32)],
        compiler_params=_params("arbitrary"), name="router")(x, w_pad, b_pad)


def _dispatch_kernel(dest_ref, x_ref, zero_ref, xs_ref, sem, *, tt):
    del zero_ref

    def copy(n, kk):
        return pltpu.make_async_copy(x_ref.at[pl.ds(n, 1)], xs_ref.at[pl.ds(dest_ref[n * TOP_K + kk], 1)], sem)

    def start(n, carry):
        for kk in range(TOP_K):
            copy(n, kk).start(priority=kk % 2)
        return carry

    def wait(n, carry):
        for kk in range(TOP_K):
            copy(n, kk).wait()
        return carry

    lax.fori_loop(0, tt, start, 0)
    lax.fori_loop(0, tt, wait, 0)


def _dispatch(x, dest_flat, rows):
    m, d = x.shape
    tt = _tile(m, 128, 8)
    return pl.pallas_call(
        functools.partial(_dispatch_kernel, tt=tt), grid=(m // tt,),
        in_specs=[pl.BlockSpec((tt * TOP_K,), lambda i: (i,), memory_space=pltpu.SMEM),
                  pl.BlockSpec((tt, d), lambda i: (i, 0)),
                  pl.BlockSpec(memory_space=pl.ANY)],
        out_specs=pl.BlockSpec(memory_space=pl.ANY),
        out_shape=jax.ShapeDtypeStruct((rows, d), F32),
        scratch_shapes=[pltpu.SemaphoreType.DMA(())],
        input_output_aliases={2: 0},
        compiler_params=_params("arbitrary"), name="moe_dispatch")(dest_flat, x, jnp.zeros((rows, d), F32))


def _expert_kernel(be_ref, nu_ref, xs_ref, wgu_ref, bgu_ref, wd_ref, bd_ref, y_ref, wgu_bf, wd_bf, *, d_ff):
    i = pl.program_id(0)
    prev = be_ref[jnp.maximum(i - 1, 0)]

    @pl.when((i == 0) | (be_ref[i] != prev))
    def _():
        wgu_bf[...] = wgu_ref[0, 0].astype(BF16)
        wd_bf[...] = wd_ref[0, 0].astype(BF16)

    @pl.when(i < nu_ref[0])
    def _():
        gu = _dot(xs_ref[...].astype(BF16), wgu_bf[...]) + bgu_ref[0]
        x_glu = jnp.minimum(gu[:, :d_ff], SWIGLU_LIMIT)
        x_lin = jnp.clip(gu[:, d_ff:], -SWIGLU_LIMIT, SWIGLU_LIMIT)
        act = x_glu * _sigmoid(SWIGLU_ALPHA * x_glu) * (x_lin + 1.0)
        y_ref[...] = _dot(act.astype(BF16), wd_bf[...]) + bd_ref[0]

    @pl.when(i >= nu_ref[0])
    def _():
        y_ref[...] = jnp.zeros_like(y_ref)


def _experts(xs, block_expert, n_used, w_gate_up, b_gate_up, w_down, b_down, layer):
    rows, d = xs.shape
    _, n_exp, _, ff2 = w_gate_up.shape
    d_ff = ff2 // 2
    n_blocks = rows // MOE_ROWS
    blk = lambda i, be, nu: (jnp.minimum(i, nu[0] - 1), 0)
    grid_spec = pltpu.PrefetchScalarGridSpec(
        num_scalar_prefetch=2, grid=(n_blocks,),
        in_specs=[pl.BlockSpec((MOE_ROWS, d), blk),
                  pl.BlockSpec((1, 1, d, ff2), lambda i, be, nu: (layer, be[i], 0, 0)),
                  pl.BlockSpec((1, 1, ff2), lambda i, be, nu: (be[i], 0, 0)),
                  pl.BlockSpec((1, 1, d_ff, d), lambda i, be, nu: (layer, be[i], 0, 0)),
                  pl.BlockSpec((1, 1, d), lambda i, be, nu: (be[i], 0, 0))],
        out_specs=pl.BlockSpec((MOE_ROWS, d), lambda i, be, nu: (i, 0)),
        scratch_shapes=[pltpu.VMEM((d, ff2), BF16), pltpu.VMEM((d_ff, d), BF16)])
    return pl.pallas_call(
        functools.partial(_expert_kernel, d_ff=d_ff), grid_spec=grid_spec,
        out_shape=jax.ShapeDtypeStruct((rows, d), F32),
        compiler_params=_params("arbitrary"), name="moe_experts")(
            block_expert, n_used, xs, w_gate_up, b_gate_up.reshape(n_exp, 1, ff2),
            w_down, b_down.reshape(n_exp, 1, d))


def _combine_kernel(dest_ref, wt_ref, h_ref, lg_ref, lb_ref, y_ref, o_ref, ob_ref, rows, sem, *, tt, alpha):
    def copy(n, kk):
        return pltpu.make_async_copy(y_ref.at[pl.ds(dest_ref[n * TOP_K + kk], 1)], rows.at[kk, pl.ds(n, 1)], sem)

    def start(n, carry):
        for kk in range(TOP_K):
            copy(n, kk).start(priority=kk % 2)
        return carry

    def wait(n, carry):
        for kk in range(TOP_K):
            copy(n, kk).wait()
        return carry

    lax.fori_loop(0, tt, start, 0)
    lax.fori_loop(0, tt, wait, 0)
    wt = wt_ref[...]
    moe = rows[0] * wt[:, 0:1]
    for kk in range(1, TOP_K):
        moe = moe + rows[kk] * wt[:, kk:kk + 1]
    y = _layer_norm(alpha * h_ref[...] + moe, lg_ref[...], lb_ref[...])
    o_ref[...] = y
    ob_ref[...] = y.astype(BF16)


def _combine(y, dest_flat, wt, h, ln_g, ln_b, alpha):
    m, d = h.shape
    tt = _tile(m, 128, 16)
    vec = pl.BlockSpec((1, d), lambda i: (0, 0))
    row = pl.BlockSpec((tt, d), lambda i: (i, 0))
    return pl.pallas_call(
        functools.partial(_combine_kernel, tt=tt, alpha=alpha), grid=(m // tt,),
        in_specs=[pl.BlockSpec((tt * TOP_K,), lambda i: (i,), memory_space=pltpu.SMEM),
                  pl.BlockSpec((tt, LANES), lambda i: (i, 0)), row, vec, vec,
                  pl.BlockSpec(memory_space=pl.ANY)],
        out_specs=[row, row],
        out_shape=[jax.ShapeDtypeStruct((m, d), F32), jax.ShapeDtypeStruct((m, d), BF16)],
        scratch_shapes=[pltpu.VMEM((TOP_K, tt, d), F32), pltpu.SemaphoreType.DMA(())],
        compiler_params=_params("arbitrary"), name="moe_combine")(dest_flat, wt, h, ln_g, ln_b, y)


def _moe(h, w, alpha, layer):
    m, d = h.shape
    n_exp = w["n_exp"]
    idx, wt, rank, counts = _router(h, w["w_router_pad"], w["b_router_pad"], n_exp)
    counts = counts[0, :n_exp]
    padded = (counts + MOE_ROWS - 1) // MOE_ROWS * MOE_ROWS
    p_end = jnp.cumsum(padded)
    p_start = p_end - padded
    dest = (p_start[idx[:, :TOP_K]] + rank[:, :TOP_K]).reshape(-1)
    n_blocks = -(-(m * TOP_K) // MOE_ROWS) + n_exp
    block_start = jnp.arange(n_blocks, dtype=jnp.int32) * MOE_ROWS
    block_expert = jnp.minimum(jnp.sum(p_end[None, :] <= block_start[:, None], axis=1), n_exp - 1).astype(jnp.int32)
    n_used = (p_end[-1:] // MOE_ROWS).astype(jnp.int32)
    xs = _dispatch(h, dest, n_blocks * MOE_ROWS)
    y = _experts(xs, block_expert, n_used, w["w_gate_up"], w["b_gate_up"], w["w_down"], w["b_down"], layer)
    return _combine(y, dest, wt, h, w["ln2_g"], w["ln2_b"], alpha)


def _layer(h, hb, w, dims, alpha, states, layer):
    nb, t, d, gh, gdk, gdv, hh, hdk, hdv = dims
    z = _matmul(hb, w["w_in_main"])
    la = _gla_decay(hb, w["w_lra"], w["w_gla_lr"], w["b_gla_lr"])
    col_conv = (2 * gh * gdk + 2 * gh * gdv) // d
    col_hgrn = col_conv + 5
    col_gate = col_hgrn + 4
    if states is None:
        oa, s_gla = _gla_prompt(z, la, w["gla_norm_g"], nb, t, gh, gdk, gdv)
        od, s_hgrn = _hgrn_prompt(z, w["hgrn_lb"], w["hgrn_norm_g"], nb, t, hh, hdk, hdv, col_hgrn)
        ob, oc, c_s, c_c = _conv_prompt(z, w["sconv_w"], w["conf_conv_w"], w["conf_conv_b"],
                                        w["conf_ln_g"], w["conf_ln_b"], nb, t, d, col_conv)
    else:
        st_gla, st_hgrn, new_gla, new_hgrn, cache_s, cache_c = states
        oa, s_gla = _gla_step(z, la, w["gla_norm_g"], st_gla, new_gla, layer)
        od, s_hgrn = _hgrn_step(z, w["hgrn_lb"], w["hgrn_norm_g"], st_hgrn, new_hgrn, layer, col_hgrn)
        ob, oc, u, uc = _conv_step(z, cache_s, cache_c, w["sconv_w"], w["conf_conv_w"], w["conf_conv_b"],
                                   w["conf_ln_g"], w["conf_ln_b"], d, col_conv)
        c_s = jnp.concatenate([cache_s[:, 1:], u[:, None]], axis=1)
        c_c = jnp.concatenate([cache_c[:, 1:], uc[:, None]], axis=1)
    h1 = _merge(oa, ob, oc, od, z, h, w["w_br_a"], w["w_br_b"], w["w_br_c"], w["w_br_d"], w["w_o"],
                w["ln1_g"], w["ln1_b"], alpha, col_gate)
    h2, h2b = _moe(h1, w, alpha, layer)
    return h2, h2b, (s_gla, s_hgrn, c_s, c_c)


def _trunk(x, states, ln_in_g, ln_in_b, layers, dims, alpha):
    nb, t, d = dims[:3]
    h, hb = _ln_call(x.reshape(nb * t, d), ln_in_g, ln_in_b)
    if states is None:
        new = ([], [], [], [])
        for l, w in enumerate(layers):
            h, hb, out = _layer(h, hb, w, dims, alpha, None, l)
            for acc, s in zip(new, out):
                acc.append(s)
        return h.reshape(nb, t, d), tuple(jnp.stack(a) for a in new)
    st_gla, st_hgrn, cache_s, cache_c = states
    new_gla, new_hgrn = jnp.zeros_like(st_gla), jnp.zeros_like(st_hgrn)
    new_s, new_c = [], []
    for l, w in enumerate(layers):
        h, hb, (new_gla, new_hgrn, c_s, c_c) = _layer(
            h, hb, w, dims, alpha, (st_gla, st_hgrn, new_gla, new_hgrn, cache_s[l], cache_c[l]), l)
        new_s.append(c_s)
        new_c.append(c_c)
    return h.reshape(nb, t, d), (new_gla, new_hgrn, jnp.stack(new_s), jnp.stack(new_c))


def kernel(x_prompt, x_sample, state_gla, state_hgrn, cache_sconv, cache_conformer, ln_in_g, ln_in_b, w_in, w_gla_lr, b_gla_lr, gla_norm_g, w_br_a, sconv_w, w_br_b, conf_conv_w, conf_conv_b, conf_ln_g, conf_ln_b, w_br_c, hgrn_lb_logits, hgrn_norm_g, w_br_d, w_o, ln1_g, ln1_b, w_router, b_router, w_gate_up, b_gate_up, w_down, b_down, ln2_g, ln2_b):
    depth, d, _ = w_in.shape
    _, _, gh, gdk, gdv = state_gla.shape
    _, _, hh, hdk, hdv = state_hgrn.shape
    rank = w_gla_lr.shape[1]
    n_exp = w_router.shape[2]
    alpha = (2 * depth) ** 0.25
    lra0 = 2 * gh * gdk + 2 * gh * gdv

    lb_p = jax.nn.softmax(hgrn_lb_logits.astype(F32), axis=0)
    hgrn_lb = jnp.cumsum(lb_p, axis=0) - lb_p[:1]
    row = lambda a: a.reshape(1, -1)
    layers = []
    for l in range(depth):
        w_lra = jnp.zeros((d, LANES), F32).at[:, :rank].set(w_in[l, :, lra0:lra0 + rank])
        w2 = jnp.zeros((LANES, gh * gdk), F32).at[:rank].set(w_gla_lr[l])
        layers.append({
            "w_in_main": jnp.concatenate([w_in[l, :, :lra0], w_in[l, :, lra0 + rank:]], axis=1).astype(BF16),
            "w_lra": w_lra.astype(BF16), "w_gla_lr": w2.astype(BF16), "b_gla_lr": row(b_gla_lr[l]),
            "gla_norm_g": row(gla_norm_g[l]), "hgrn_norm_g": row(hgrn_norm_g[l]), "hgrn_lb": row(hgrn_lb[l]),
            "sconv_w": sconv_w[l], "conf_conv_w": conf_conv_w[l], "conf_conv_b": row(conf_conv_b[l]),
            "conf_ln_g": row(conf_ln_g[l]), "conf_ln_b": row(conf_ln_b[l]),
            "w_br_a": w_br_a[l].astype(BF16), "w_br_b": w_br_b[l].astype(BF16),
            "w_br_c": w_br_c[l].astype(BF16), "w_br_d": w_br_d[l].astype(BF16), "w_o": w_o[l].astype(BF16),
            "ln1_g": row(ln1_g[l]), "ln1_b": row(ln1_b[l]), "ln2_g": row(ln2_g[l]), "ln2_b": row(ln2_b[l]),
            "n_exp": n_exp,
            "w_router_pad": jnp.zeros((d, LANES), F32).at[:, :n_exp].set(w_router[l]),
            "b_router_pad": jnp.zeros((1, LANES), F32).at[0, :n_exp].set(b_router[l]),
            "w_gate_up": w_gate_up, "b_gate_up": b_gate_up[l], "w_down": w_down, "b_down": b_down[l],
        })

    nbp, tp, _ = x_prompt.shape
    nbs, ts, _ = x_sample.shape
    assert ts == 1, "the sample group advances its states by exactly one token"
    y_p, st_p = _trunk(x_prompt, None, ln_in_g, ln_in_b, layers, (nbp, tp, d, gh, gdk, gdv, hh, hdk, hdv), alpha)
    y_s, st_s = _trunk(x_sample, (state_gla, state_hgrn, cache_sconv, cache_conformer), ln_in_g, ln_in_b,
                       layers, (nbs, ts, d, gh, gdk, gdv, hh, hdk, hdv), alpha)
    return (y_p, y_s) + st_p + st_s
```

```python
import functools

import jax
import jax.numpy as jnp
from jax import lax
from jax.experimental import pallas as pl
from jax.experimental.pallas import tpu as pltpu

F32 = jnp.float32
BF16 = jnp.bfloat16

TOP_K = 4
CHUNK = 64
SUB = 16
GLA_GATE_NORM = 16.0
F_MIN = 1e-20
LN_EPS = 1e-5
RMS_EPS = 1e-6
SWIGLU_ALPHA = 1.702
SWIGLU_LIMIT = 7.0
LOG2_E = 1.4426950408889634
MOE_ROWS = 512
LANES = 128
SUBLANES = 8
CONV_ROWS = 32
CONF_HIST = 32
SCONV_HIST = 8
V7X_VMEM_LIMIT = 56 * 1024 * 1024


def _params(*sem):
    return pltpu.CompilerParams(dimension_semantics=sem, vmem_limit_bytes=V7X_VMEM_LIMIT)


def _tile(n, pref, mult=8):
    if n <= pref:
        return n
    t = pref - pref % mult
    while t >= mult:
        if n % t == 0:
            return t
        t -= mult
    raise ValueError(f"no tile for {n}")


def _sigmoid(x):
    return 1.0 / (1.0 + jnp.exp(-x))


def _silu(x):
    return x * _sigmoid(x)


def _log_sigmoid(x):
    return jnp.minimum(x, 0.0) - jnp.log(1.0 + jnp.exp(-jnp.abs(x)))


def _layer_norm(x, g, b):
    mu = jnp.mean(x, axis=-1, keepdims=True)
    xc = x - mu
    var = jnp.mean(xc * xc, axis=-1, keepdims=True)
    return xc * lax.rsqrt(var + LN_EPS) * g + b


def _dot(a, b):
    return jnp.dot(a, b, preferred_element_type=F32)


def _dot_nt(a, b):
    return lax.dot_general(a, b, (((1,), (1,)), ((), ())), preferred_element_type=F32)


def _dot_tn(a, b):
    return lax.dot_general(a, b, (((0,), (0,)), ((), ())), preferred_element_type=F32)


def _split3(x):
    hi = x.astype(BF16)
    r = x - hi.astype(F32)
    mid = r.astype(BF16)
    lo = (r - mid.astype(F32)).astype(BF16)
    return hi, mid, lo


def _ln_kernel(x_ref, g_ref, b_ref, o_ref, ob_ref):
    y = _layer_norm(x_ref[...], g_ref[...], b_ref[...])
    o_ref[...] = y
    ob_ref[...] = y.astype(BF16)


def _ln_call(x, g, b):
    m, d = x.shape
    tm = _tile(m, 512, 16)
    row = pl.BlockSpec((tm, d), lambda i: (i, 0))
    vec = pl.BlockSpec((1, d), lambda i: (0, 0))
    return pl.pallas_call(
        _ln_kernel, grid=(m // tm,), in_specs=[row, vec, vec], out_specs=[row, row],
        out_shape=[jax.ShapeDtypeStruct((m, d), F32), jax.ShapeDtypeStruct((m, d), BF16)],
        compiler_params=_params("parallel"), name="ln_in")(x, g.reshape(1, d), b.reshape(1, d))


def _mm_kernel(x_ref, w_ref, o_ref):
    o_ref[...] = _dot(x_ref[...], w_ref[0])


def _matmul(x, w, layer, row0, m):
    k = x.shape[1]
    n = w.shape[2]
    tm = _tile(m, 1024, 16)
    tn = _tile(n, 2048, LANES)
    assert row0 % tm == 0
    r0 = row0 // tm
    return pl.pallas_call(
        _mm_kernel, grid=(n // tn, m // tm),
        in_specs=[pl.BlockSpec((tm, k), lambda j, i: (r0 + i, 0)),
                  pl.BlockSpec((1, k, tn), lambda j, i: (layer, 0, j))],
        out_specs=pl.BlockSpec((tm, tn), lambda j, i: (i, j)),
        out_shape=jax.ShapeDtypeStruct((m, n), F32),
        compiler_params=_params("parallel", "parallel"), name="in_proj")(x, w)


def _gla_decay_kernel(h_ref, w1_ref, w2_ref, b2_ref, o_ref):
    lra = _dot(h_ref[...], w1_ref[...])
    x = _dot(lra.astype(BF16), w2_ref[...]) + b2_ref[...]
    o_ref[...] = _log_sigmoid(x) * (1.0 / GLA_GATE_NORM)


def _gla_decay(hb, w1, w2, b2, row0, m):
    d = hb.shape[1]
    n = w2.shape[1]
    tm = _tile(m, 512, 16)
    assert row0 % tm == 0
    r0 = row0 // tm
    return pl.pallas_call(
        _gla_decay_kernel, grid=(m // tm,),
        in_specs=[pl.BlockSpec((tm, d), lambda i: (r0 + i, 0)), pl.BlockSpec(w1.shape, lambda i: (0, 0)),
                  pl.BlockSpec(w2.shape, lambda i: (0, 0)), pl.BlockSpec((1, n), lambda i: (0, 0))],
        out_specs=pl.BlockSpec((tm, n), lambda i: (i, 0)),
        out_shape=jax.ShapeDtypeStruct((m, n), F32),
        compiler_params=_params("parallel"), name="gla_decay")(hb, w1, w2, b2)


def _cumsum_rows(x):
    c = x.shape[0]
    r = lax.broadcasted_iota(jnp.int32, (c, c), 0)
    s = lax.broadcasted_iota(jnp.int32, (c, c), 1)
    tri = jnp.where(r >= s, 1.0, 0.0).astype(BF16)
    hi, mid, lo = _split3(x)
    return _dot(tri, hi) + _dot(tri, mid) + _dot(tri, lo)


def _intra_scores(q, k, b2):
    c, dk = q.shape
    half = SUB // 2
    lane = lax.broadcasted_iota(jnp.int32, (half, c), 1)
    row = lax.broadcasted_iota(jnp.int32, (half, c), 0)
    blocks = []
    for i in range(c // SUB):
        lo = i * SUB
        qi, ki, bi = q[lo:lo + SUB], k[lo:lo + SUB], b2[lo:lo + SUB]
        if i > 0:
            ref = b2[lo - 1:lo]
            qs = qi * jnp.exp2(bi - ref)
            ks = jnp.concatenate([k[:lo] * jnp.exp2(ref - b2[:lo]), jnp.zeros((c - lo, dk), F32)], axis=0)
            a = _dot_nt(qs.astype(BF16), ks.astype(BF16))
            top, bot = a[:half], a[half:]
        else:
            top = bot = jnp.zeros((half, c), F32)
        for s in range(SUB):
            ks_row, bs_row = ki[s:s + 1], bi[s:s + 1]
            if s < half:
                p = qi[:half] * ks_row * jnp.exp2(bi[:half] - bs_row)
                top = jnp.where((lane == lo + s) & (row >= s), jnp.sum(p, axis=-1, keepdims=True), top)
            p = qi[half:] * ks_row * jnp.exp2(bi[half:] - bs_row)
            keep = (lane == lo + s) if s < half else (lane == lo + s) & (row >= s - half)
            bot = jnp.where(keep, jnp.sum(p, axis=-1, keepdims=True), bot)
        blocks += [top, bot]
    return jnp.concatenate(blocks, axis=0)


def _chunk_step(q, k, v, b, st):
    c = q.shape[0]
    b2 = b * LOG2_E
    b_end = b2[c - 1:c]
    o = _dot_nt((q * jnp.exp2(b2)).astype(BF16), st.astype(BF16))
    a = _intra_scores(q, k, b2)
    vb = v.astype(BF16)
    o = o + _dot(a.astype(BF16), vb)
    kb = (k * jnp.exp2(b_end - b2)).astype(BF16)
    st_new = st * jnp.exp2(b_end) + _dot_tn(vb, kb)
    return o, st_new


def _gated_rms(o, gate, g):
    o = o * lax.rsqrt(jnp.mean(o * o, axis=-1, keepdims=True) + RMS_EPS) * g
    return o * _silu(gate)


def _gla_kernel(q_ref, k_ref, v_ref, g_ref, la_ref, ng_ref, o_ref, so_ref, st_ref, *, heads, dk, dv):
    j = pl.program_id(1)

    @pl.when(j == 0)
    def _():
        st_ref[...] = jnp.zeros_like(st_ref)

    b_all = _cumsum_rows(la_ref[...])
    scale = dk ** -0.5
    for h in range(heads):
        ks = slice(h * dk, (h + 1) * dk)
        vs = slice(h * dv, (h + 1) * dv)
        o, st = _chunk_step(q_ref[:, ks] * scale, k_ref[:, ks], v_ref[:, vs], b_all[:, ks], st_ref[h])
        st_ref[h] = st
        o_ref[:, vs] = _gated_rms(o, g_ref[:, vs], ng_ref[...]).astype(BF16)

    @pl.when(j == pl.num_programs(1) - 1)
    def _():
        for h in range(heads):
            so_ref[0, h] = st_ref[h].T


def _gla_prompt(z, la, norm_g, nb, t, heads, dk, dv):
    c = min(CHUNK, t)
    n = t // c
    wk, wv = heads * dk, heads * dv
    row = lambda col: (lambda bi, j: (bi * n + j, col))
    kern = functools.partial(_gla_kernel, heads=heads, dk=dk, dv=dv)
    return pl.pallas_call(
        kern, grid=(nb, n),
        in_specs=[pl.BlockSpec((c, wk), row(0)), pl.BlockSpec((c, wk), row(1)),
                  pl.BlockSpec((c, wv), row(1)), pl.BlockSpec((c, wv), row(2)),
                  pl.BlockSpec((c, wk), row(0)), pl.BlockSpec((1, dv), lambda bi, j: (0, 0))],
        out_specs=[pl.BlockSpec((c, wv), row(0)),
                   pl.BlockSpec((1, heads, dk, dv), lambda bi, j: (bi, 0, 0, 0))],
        out_shape=[jax.ShapeDtypeStruct((nb * t, wv), BF16),
                   jax.ShapeDtypeStruct((nb, heads, dk, dv), F32)],
        scratch_shapes=[pltpu.VMEM((heads, dv, dk), F32)],
        compiler_params=_params("parallel", "arbitrary"), name="gla_prompt")(z, z, z, z, la, norm_g)


def _hgrn_inputs(qd, fd, lb):
    f = lb + (1.0 - lb) * _sigmoid(fd)
    return _silu(qd), 1.0 - f, jnp.log(jnp.maximum(f, F_MIN))


def _hgrn_kernel(q_ref, f_ref, v_ref, g_ref, lb_ref, ng_ref, o_ref, so_ref, st_ref, *, heads, dk, dv):
    j = pl.program_id(1)

    @pl.when(j == 0)
    def _():
        st_ref[...] = jnp.zeros_like(st_ref)

    q_all, k_all, lf = _hgrn_inputs(q_ref[...], f_ref[...], lb_ref[...])
    b_all = _cumsum_rows(lf)
    for h in range(heads):
        ks = slice(h * dk, (h + 1) * dk)
        vs = slice(h * dv, (h + 1) * dv)
        o, st = _chunk_step(q_all[:, ks], k_all[:, ks], v_ref[:, vs], b_all[:, ks], st_ref[h])
        st_ref[h] = st
        o_ref[:, vs] = _gated_rms(o, g_ref[:, vs], ng_ref[...]).astype(BF16)

    @pl.when(j == pl.num_programs(1) - 1)
    def _():
        for h in range(heads):
            so_ref[0, h] = st_ref[h].T


def _hgrn_prompt(z, lb, norm_g, nb, t, heads, dk, dv, col0):
    c = min(CHUNK, t)
    n = t // c
    w = heads * dk
    row = lambda col: (lambda bi, j: (bi * n + j, col0 + col))
    kern = functools.partial(_hgrn_kernel, heads=heads, dk=dk, dv=dv)
    return pl.pallas_call(
        kern, grid=(nb, n),
        in_specs=[pl.BlockSpec((c, w), row(0)), pl.BlockSpec((c, w), row(1)),
                  pl.BlockSpec((c, w), row(2)), pl.BlockSpec((c, w), row(3)),
                  pl.BlockSpec((1, w), lambda bi, j: (0, 0)), pl.BlockSpec((1, dv), lambda bi, j: (0, 0))],
        out_specs=[pl.BlockSpec((c, w), lambda bi, j: (bi * n + j, 0)),
                   pl.BlockSpec((1, heads, dk, dv), lambda bi, j: (bi, 0, 0, 0))],
        out_shape=[jax.ShapeDtypeStruct((nb * t, w), BF16),
                   jax.ShapeDtypeStruct((nb, heads, dk, dv), F32)],
        scratch_shapes=[pltpu.VMEM((heads, dv, dk), F32)],
        compiler_params=_params("parallel", "arbitrary"), name="hgrn_prompt")(z, z, z, z, lb, norm_g)


def _columns(x, n):
    w = x.shape[1]
    pad = jnp.concatenate([x, jnp.zeros((w - n, w), F32)], axis=0) if n < w else x
    return pad.T


def _step_heads(q_all, k_all, a_all, v_ref, g_ref, ng_ref, s_ref, o_ref, so_ref, *, heads, dk, dv, tb):
    for h in range(heads):
        ks = slice(h * dk, (h + 1) * dk)
        vs = slice(h * dv, (h + 1) * dv)
        qc, kc, ac = _columns(q_all[:, ks], tb), _columns(k_all[:, ks], tb), _columns(a_all[:, ks], tb)
        v = v_ref[:, vs]
        row = lax.broadcasted_iota(jnp.int32, (tb, dv), 0)
        o = jnp.zeros((tb, dv), F32)
        for n in range(tb):
            s_new = ac[:, n:n + 1] * s_ref[0, n, h] + kc[:, n:n + 1] * v[n:n + 1]
            so_ref[0, n, h] = s_new
            o = jnp.where(row == n, jnp.sum(qc[:, n:n + 1] * s_new, axis=0, keepdims=True), o)
        o_ref[:, vs] = _gated_rms(o, g_ref[:, vs], ng_ref[...]).astype(BF16)


def _gla_step_kernel(q_ref, k_ref, v_ref, g_ref, la_ref, ng_ref, s_ref, prev_ref, o_ref, so_ref,
                     *, heads, dk, dv, tb):
    del prev_ref
    _step_heads(q_ref[...] * dk ** -0.5, k_ref[...], jnp.exp(la_ref[...]), v_ref, g_ref, ng_ref, s_ref,
                o_ref, so_ref, heads=heads, dk=dk, dv=dv, tb=tb)


def _hgrn_step_kernel(q_ref, f_ref, v_ref, g_ref, lb_ref, ng_ref, s_ref, prev_ref, o_ref, so_ref,
                      *, heads, dk, dv, tb):
    del prev_ref
    q_all, k_all, lf = _hgrn_inputs(q_ref[...], f_ref[...], lb_ref[...])
    _step_heads(q_all, k_all, jnp.exp(lf), v_ref, g_ref, ng_ref, s_ref, o_ref, so_ref,
                heads=heads, dk=dk, dv=dv, tb=tb)


def _gla_step(z, la, norm_g, states, new_states, layer):
    m = z.shape[0]
    _, _, heads, dk, dv = states.shape
    tb = 8
    wk, wv = heads * dk, heads * dv
    row = lambda col: (lambda i: (i, col))
    st = pl.BlockSpec((1, tb, heads, dk, dv), lambda i: (layer, i, 0, 0, 0))
    kern = functools.partial(_gla_step_kernel, heads=heads, dk=dk, dv=dv, tb=tb)
    return pl.pallas_call(
        kern, grid=(m // tb,),
        in_specs=[pl.BlockSpec((tb, wk), row(0)), pl.BlockSpec((tb, wk), row(1)),
                  pl.BlockSpec((tb, wv), row(1)), pl.BlockSpec((tb, wv), row(2)),
                  pl.BlockSpec((tb, wk), row(0)), pl.BlockSpec((1, dv), lambda i: (0, 0)), st,
                  pl.BlockSpec(memory_space=pl.ANY)],
        out_specs=[pl.BlockSpec((tb, wv), row(0)), st],
        out_shape=[jax.ShapeDtypeStruct((m, wv), BF16), jax.ShapeDtypeStruct(states.shape, F32)],
        input_output_aliases={7: 1},
        compiler_params=_params("parallel"), name="gla_step")(z, z, z, z, la, norm_g, states, new_states)


def _hgrn_step(z, lb, norm_g, states, new_states, layer, col0):
    m = z.shape[0]
    _, _, heads, dk, dv = states.shape
    tb = 8
    w = heads * dk
    row = lambda col: (lambda i: (i, col0 + col))
    st = pl.BlockSpec((1, tb, heads, dk, dv), lambda i: (layer, i, 0, 0, 0))
    kern = functools.partial(_hgrn_step_kernel, heads=heads, dk=dk, dv=dv, tb=tb)
    return pl.pallas_call(
        kern, grid=(m // tb,),
        in_specs=[pl.BlockSpec((tb, w), row(0)), pl.BlockSpec((tb, w), row(1)),
                  pl.BlockSpec((tb, w), row(2)), pl.BlockSpec((tb, w), row(3)),
                  pl.BlockSpec((1, w), lambda i: (0, 0)), pl.BlockSpec((1, dv), lambda i: (0, 0)), st,
                  pl.BlockSpec(memory_space=pl.ANY)],
        out_specs=[pl.BlockSpec((tb, w), lambda i: (i, 0)), st],
        out_shape=[jax.ShapeDtypeStruct((m, w), BF16), jax.ShapeDtypeStruct(states.shape, F32)],
        input_output_aliases={7: 1},
        compiler_params=_params("parallel"), name="hgrn_step")(z, z, z, z, lb, norm_g, states, new_states)


def _taps(win, w_ref, lanes, first, n_taps):
    n = win.shape[0]
    acc = None
    for res in range(SUBLANES):
        offs = [o for o in range(first, first + n_taps) if o % SUBLANES == res]
        if not offs:
            continue
        sh = win if res == 0 else pltpu.roll(win, n - res, axis=0)
        for o in offs:
            base = o - res
            term = w_ref[o - first:o - first + 1, lanes] * sh[base:base + CONV_ROWS]
            acc = term if acc is None else acc + term
    return acc


def _conv_kernel(gb_ref, gc_ref, hb_ref, ga_ref, gs_ref, sw_ref, cw_ref, cb_ref, lg_ref, lb_ref,
                 ob_ref, oc_ref, so_ref, co_ref, ubuf, cbuf, *, tt, sw, cw):
    t = pl.program_id(1)

    @pl.when(t == 0)
    def _():
        ubuf[0:SCONV_HIST] = jnp.zeros((SCONV_HIST, ubuf.shape[1]), F32)
        cbuf[0:CONF_HIST] = jnp.zeros((CONF_HIST, cbuf.shape[1]), F32)

    ubuf[SCONV_HIST:SCONV_HIST + tt] = gc_ref[...] * hb_ref[...]
    cbuf[CONF_HIST:CONF_HIST + tt] = ga_ref[...] * _sigmoid(gs_ref[...])
    s0 = SCONV_HIST - (sw - 1)
    c0 = CONF_HIST - (cw - 1)

    def strip(r, carry):
        r0 = pl.multiple_of(r * CONV_ROWS, CONV_ROWS)
        yb, yc = [], []
        for lb in range(ubuf.shape[1] // LANES):
            ls = slice(lb * LANES, (lb + 1) * LANES)
            yb.append(_taps(ubuf[pl.ds(r0, CONV_ROWS + SCONV_HIST), ls], sw_ref, ls, s0, sw))
            yc.append(_taps(cbuf[pl.ds(r0, CONV_ROWS + CONF_HIST), ls], cw_ref, ls, c0, cw))
        acc = jnp.concatenate(yb, axis=1)
        ob_ref[pl.ds(r0, CONV_ROWS)] = (gb_ref[pl.ds(r0, CONV_ROWS)] * acc).astype(BF16)
        acc = jnp.concatenate(yc, axis=1)
        y = _layer_norm(acc + cb_ref[...], lg_ref[...], lb_ref[...])
        oc_ref[pl.ds(r0, CONV_ROWS)] = _silu(y).astype(BF16)
        return carry

    lax.fori_loop(0, tt // CONV_ROWS, strip, 0)

    @pl.when(t == pl.num_programs(1) - 1)
    def _():
        so_ref[0] = ubuf[SCONV_HIST + tt - (sw - 1):SCONV_HIST + tt]
        co_ref[0] = cbuf[CONF_HIST + tt - (cw - 1):CONF_HIST + tt]

    ubuf[0:SCONV_HIST] = ubuf[tt:tt + SCONV_HIST]
    cbuf[0:CONF_HIST] = cbuf[tt:tt + CONF_HIST]


def _conv_prompt(z, sconv_w, conf_w, conf_b, ln_g, ln_b, nb, t, d, col0):
    sw, cw = sconv_w.shape[0], conf_w.shape[0]
    tt = _tile(t, 256, CONF_HIST)
    n = t // tt
    row = lambda col: (lambda bi, j: (bi * n + j, col0 + col))
    vec = lambda r: pl.BlockSpec((r, d), lambda bi, j: (0, 0))
    kern = functools.partial(_conv_kernel, tt=tt, sw=sw, cw=cw)
    blk = lambda col: pl.BlockSpec((tt, d), row(col))
    return pl.pallas_call(
        kern, grid=(nb, n),
        in_specs=[blk(0), blk(1), blk(2), blk(3), blk(4), vec(sw), vec(cw), vec(1), vec(1), vec(1)],
        out_specs=[pl.BlockSpec((tt, d), lambda bi, j: (bi * n + j, 0)),
                   pl.BlockSpec((tt, d), lambda bi, j: (bi * n + j, 0)),
                   pl.BlockSpec((1, sw - 1, d), lambda bi, j: (bi, 0, 0)),
                   pl.BlockSpec((1, cw - 1, d), lambda bi, j: (bi, 0, 0))],
        out_shape=[jax.ShapeDtypeStruct((nb * t, d), BF16), jax.ShapeDtypeStruct((nb * t, d), BF16),
                   jax.ShapeDtypeStruct((nb, sw - 1, d), F32), jax.ShapeDtypeStruct((nb, cw - 1, d), F32)],
        scratch_shapes=[pltpu.VMEM((SCONV_HIST + tt, d), F32), pltpu.VMEM((CONF_HIST + tt, d), F32)],
        compiler_params=_params("parallel", "arbitrary"), name="conv_prompt")(
            z, z, z, z, z, sconv_w, conf_w, conf_b, ln_g, ln_b)


def _conv_step_kernel(gb_ref, gc_ref, hb_ref, ga_ref, gs_ref, sc_ref, cc_ref, sw_ref, cw_ref, cb_ref,
                      lg_ref, lb_ref, ob_ref, oc_ref, u_ref, uc_ref, *, tb, sw, cw):
    u = gc_ref[...] * hb_ref[...]
    uc = ga_ref[...] * _sigmoid(gs_ref[...])
    u_ref[...] = u
    uc_ref[...] = uc
    row = lax.broadcasted_iota(jnp.int32, u.shape, 0)
    ys = jnp.zeros_like(u)
    yc = jnp.zeros_like(u)
    for n in range(tb):
        ys = jnp.where(row == n, jnp.sum(sc_ref[n] * sw_ref[0:sw - 1], axis=0, keepdims=True), ys)
        yc = jnp.where(row == n, jnp.sum(cc_ref[n] * cw_ref[0:cw - 1], axis=0, keepdims=True), yc)
    yb = ys + sw_ref[sw - 1:sw] * u
    ob_ref[...] = (gb_ref[...] * yb).astype(BF16)
    y = yc + cw_ref[cw - 1:cw] * uc + cb_ref[...]
    oc_ref[...] = _silu(_layer_norm(y, lg_ref[...], lb_ref[...])).astype(BF16)


def _conv_step(z, cache_s, cache_c, sconv_w, conf_w, conf_b, ln_g, ln_b, d, col0):
    m = z.shape[0]
    sw, cw = sconv_w.shape[0], conf_w.shape[0]
    tb = 16
    blk = lambda col: pl.BlockSpec((tb, d), lambda i: (i, col0 + col))
    vec = lambda r: pl.BlockSpec((r, d), lambda i: (0, 0))
    out = pl.BlockSpec((tb, d), lambda i: (i, 0))
    kern = functools.partial(_conv_step_kernel, tb=tb, sw=sw, cw=cw)
    return pl.pallas_call(
        kern, grid=(m // tb,),
        in_specs=[blk(0), blk(1), blk(2), blk(3), blk(4),
                  pl.BlockSpec((tb, sw - 1, d), lambda i: (i, 0, 0)),
                  pl.BlockSpec((tb, cw - 1, d), lambda i: (i, 0, 0)),
                  vec(sw), vec(cw), vec(1), vec(1), vec(1)],
        out_specs=[out, out, out, out],
        out_shape=[jax.ShapeDtypeStruct((m, d), BF16), jax.ShapeDtypeStruct((m, d), BF16),
                   jax.ShapeDtypeStruct((m, d), F32), jax.ShapeDtypeStruct((m, d), F32)],
        compiler_params=_params("parallel"), name="conv_step")(
            z, z, z, z, z, cache_s, cache_c, sconv_w, conf_w, conf_b, ln_g, ln_b)


def _merge_kernel(oa_ref, ob_ref, oc_ref, od_ref, g0_ref, g1_ref, g2_ref, g3_ref, h_ref,
                  wa_ref, wb_ref, wc_ref, wd_ref, wo_ref, lg_ref, lb_ref, prev_ref, o_ref, *, alpha):
    del prev_ref
    merged = _sigmoid(g0_ref[...]) * _dot(oa_ref[...], wa_ref[...])
    merged = merged + _sigmoid(g1_ref[...]) * _dot(ob_ref[...], wb_ref[...])
    merged = merged + _sigmoid(g2_ref[...]) * _dot(oc_ref[...], wc_ref[...])
    merged = merged + _sigmoid(g3_ref[...]) * _dot(od_ref[...], wd_ref[...])
    out = _dot(merged.astype(BF16), wo_ref[...])
    o_ref[...] = _layer_norm(alpha * h_ref[...] + out, lg_ref[...], lb_ref[...])


def _merge(oa, ob, oc, od, z, h, out, wa, wb, wc, wd, wo, ln_g, ln_b, alpha, gate_col0, row0):
    m, d = oa.shape
    tm = _tile(m, 256, 16)
    assert row0 % tm == 0
    r0 = row0 // tm
    row = pl.BlockSpec((tm, d), lambda i: (i, 0))
    mrow = pl.BlockSpec((tm, d), lambda i: (r0 + i, 0))
    gate = lambda c: pl.BlockSpec((tm, d), lambda i: (i, gate_col0 + c))
    wsp = pl.BlockSpec((d, d), lambda i: (0, 0))
    vec = pl.BlockSpec((1, d), lambda i: (0, 0))
    return pl.pallas_call(
        functools.partial(_merge_kernel, alpha=alpha), grid=(m // tm,),
        in_specs=[row, row, row, row, gate(0), gate(1), gate(2), gate(3), mrow, wsp, wsp, wsp, wsp, wsp, vec, vec,
                  pl.BlockSpec(memory_space=pl.ANY)],
        out_specs=mrow, out_shape=jax.ShapeDtypeStruct(out.shape, F32),
        input_output_aliases={16: 0},
        compiler_params=_params("parallel"), name="merge")(
            oa, ob, oc, od, z, z, z, z, h, wa, wb, wc, wd, wo, ln_g, ln_b, out)


def _router_kernel(x_ref, w_ref, b_ref, idx_ref, wt_ref, rank_ref, cnt_ref, carry_ref, *, n_exp):
    i = pl.program_id(0)

    @pl.when(i == 0)
    def _():
        carry_ref[...] = jnp.zeros_like(carry_ref)

    tm = x_ref.shape[0]
    logits = _dot(x_ref[...].astype(BF16), w_ref[...].astype(BF16)) + b_ref[...]
    lane = lax.broadcasted_iota(jnp.int32, (tm, LANES), 1)
    lane_f = lane.astype(F32)
    work = jnp.where(lane < n_exp, logits, -jnp.inf)
    vals, hots = [], []
    idx_out = jnp.zeros((tm, LANES), F32)
    for kk in range(TOP_K):
        m = jnp.max(work, axis=-1, keepdims=True)
        first = jnp.min(jnp.where(work == m, lane_f, float(LANES)), axis=-1, keepdims=True)
        hot = lane_f == first
        work = jnp.where(hot, -jnp.inf, work)
        vals.append(m)
        hots.append(hot)
        idx_out = jnp.where(lane == kk, first, idx_out)
    es = [jnp.exp(v - vals[0]) for v in vals]
    den = es[0]
    for e in es[1:]:
        den = den + e
    wt = jnp.zeros((tm, LANES), F32)
    for kk in range(TOP_K):
        wt = jnp.where(lane == kk, es[kk] / den, wt)
    chosen = hots[0]
    for hot in hots[1:]:
        chosen = chosen | hot
    onehot = jnp.where(chosen, 1.0, 0.0)
    r = lax.broadcasted_iota(jnp.int32, (tm, tm), 0)
    c = lax.broadcasted_iota(jnp.int32, (tm, tm), 1)
    before = jnp.where(c < r, 1.0, 0.0).astype(BF16)
    seen = _dot(before, onehot.astype(BF16)) + carry_ref[...]
    rank = jnp.zeros((tm, LANES), F32)
    for kk in range(TOP_K):
        rk = jnp.sum(jnp.where(hots[kk], seen, 0.0), axis=-1, keepdims=True)
        rank = jnp.where(lane == kk, rk, rank)
    idx_ref[...] = idx_out.astype(jnp.int32)
    wt_ref[...] = wt
    rank_ref[...] = rank.astype(jnp.int32)
    carry_ref[...] = carry_ref[...] + jnp.sum(onehot, axis=0, keepdims=True)
    cnt_ref[...] = carry_ref[...].astype(jnp.int32)


def _router(x, w_pad, b_pad, n_exp):
    m, d = x.shape
    tm = _tile(m, 512, 8)
    row = pl.BlockSpec((tm, LANES), lambda i: (i, 0))
    one = pl.BlockSpec((1, LANES), lambda i: (0, 0))
    return pl.pallas_call(
        functools.partial(_router_kernel, n_exp=n_exp), grid=(m // tm,),
        in_specs=[pl.BlockSpec((tm, d), lambda i: (i, 0)), pl.BlockSpec((d, LANES), lambda i: (0, 0)), one],
        out_specs=[row, row, row, one],
        out_shape=[jax.ShapeDtypeStruct((m, LANES), jnp.int32), jax.ShapeDtypeStruct((m, LANES), F32),
                   jax.ShapeDtypeStruct((m, LANES), jnp.int32), jax.ShapeDtypeStruct((1, LANES), jnp.int32)],
        scratch_shapes=[pltpu.VMEM((1, LANES), F32)],
        compiler_params=_params("arbitrary"), name="router")(x, w_pad, b_pad)


def _dispatch_kernel(pend_ref, dest_ref, x_ref, xs_ref, zbuf, sem, zsem, *, tt, n_exp):
    @pl.when(pl.program_id(0) == 0)
    def _():
        zbuf[...] = jnp.zeros_like(zbuf)

        def zero_copy(e):
            first = pl.multiple_of(pend_ref[e] - MOE_ROWS, MOE_ROWS)
            return pltpu.make_async_copy(zbuf, xs_ref.at[pl.ds(first, MOE_ROWS)], zsem)

        def nonempty(e):
            return pend_ref[e] > jnp.where(e > 0, pend_ref[jnp.maximum(e - 1, 0)], 0)

        def zstart(e, carry):
            @pl.when(nonempty(e))
            def _():
                zero_copy(e).start()
            return carry

        def zwait(e, carry):
            @pl.when(nonempty(e))
            def _():
                zero_copy(e).wait()
            return carry

        lax.fori_loop(0, n_exp, zstart, 0)
        lax.fori_loop(0, n_exp, zwait, 0)

        def tail_copy(b):
            return pltpu.make_async_copy(zbuf, xs_ref.at[pl.ds(pl.multiple_of(b * MOE_ROWS, MOE_ROWS), MOE_ROWS)], zsem)

        def tstart(b, carry):
            tail_copy(b).start()
            return carry

        def twait(b, carry):
            tail_copy(b).wait()
            return carry

        first_unused = pend_ref[n_exp - 1] // MOE_ROWS
        lax.fori_loop(first_unused, xs_ref.shape[0] // MOE_ROWS, tstart, 0)
        lax.fori_loop(first_unused, xs_ref.shape[0] // MOE_ROWS, twait, 0)

    def copy(n, kk):
        return pltpu.make_async_copy(x_ref.at[pl.ds(n, 1)], xs_ref.at[pl.ds(dest_ref[n * TOP_K + kk], 1)], sem)

    def start(n, carry):
        for kk in range(TOP_K):
            copy(n, kk).start()
        return carry

    def wait(n, carry):
        for kk in range(TOP_K):
            copy(n, kk).wait()
        return carry

    lax.fori_loop(0, tt, start, 0)
    lax.fori_loop(0, tt, wait, 0)


def _dispatch(x, dest_flat, p_end, rows):
    m, d = x.shape
    tt = _tile(m, 128, 8)
    grid_spec = pltpu.PrefetchScalarGridSpec(
        num_scalar_prefetch=1, grid=(m // tt,),
        in_specs=[pl.BlockSpec((tt * TOP_K,), lambda i, pe: (i,), memory_space=pltpu.SMEM),
                  pl.BlockSpec((tt, d), lambda i, pe: (i, 0))],
        out_specs=pl.BlockSpec(memory_space=pl.ANY),
        scratch_shapes=[pltpu.VMEM((MOE_ROWS, d), F32), pltpu.SemaphoreType.DMA(()), pltpu.SemaphoreType.DMA(())])
    return pl.pallas_call(
        functools.partial(_dispatch_kernel, tt=tt, n_exp=p_end.shape[0]), grid_spec=grid_spec,
        out_shape=jax.ShapeDtypeStruct((rows, d), F32),
        compiler_params=_params("arbitrary"), name="moe_dispatch")(p_end, dest_flat, x)


def _expert_kernel(be_ref, nu_ref, xs_ref, wgu_ref, bgu_ref, wd_ref, bd_ref, y_ref, wgu_bf, wd_bf, *, d_ff):
    i = pl.program_id(0)
    prev = be_ref[jnp.maximum(i - 1, 0)]

    @pl.when((i == 0) | (be_ref[i] != prev))
    def _():
        wgu_bf[...] = wgu_ref[0, 0].astype(BF16)
        wd_bf[...] = wd_ref[0, 0].astype(BF16)

    @pl.when(i < nu_ref[0])
    def _():
        gu = _dot(xs_ref[...].astype(BF16), wgu_bf[...]) + bgu_ref[0]
        x_glu = jnp.minimum(gu[:, :d_ff], SWIGLU_LIMIT)
        x_lin = jnp.clip(gu[:, d_ff:], -SWIGLU_LIMIT, SWIGLU_LIMIT)
        act = x_glu * _sigmoid(SWIGLU_ALPHA * x_glu) * (x_lin + 1.0)
        y_ref[...] = _dot(act.astype(BF16), wd_bf[...]) + bd_ref[0]

    @pl.when(i >= nu_ref[0])
    def _():
        y_ref[...] = jnp.zeros_like(y_ref)


def _experts(xs, block_expert, n_used, w_gate_up, b_gate_up, w_down, b_down, layer):
    rows, d = xs.shape
    _, n_exp, _, ff2 = w_gate_up.shape
    d_ff = ff2 // 2
    n_blocks = rows // MOE_ROWS
    blk = lambda i, be, nu: (jnp.minimum(i, nu[0] - 1), 0)
    grid_spec = pltpu.PrefetchScalarGridSpec(
        num_scalar_prefetch=2, grid=(n_blocks,),
        in_specs=[pl.BlockSpec((MOE_ROWS, d), blk),
                  pl.BlockSpec((1, 1, d, ff2), lambda i, be, nu: (layer, be[i], 0, 0)),
                  pl.BlockSpec((1, 1, ff2), lambda i, be, nu: (be[i], 0, 0)),
                  pl.BlockSpec((1, 1, d_ff, d), lambda i, be, nu: (layer, be[i], 0, 0)),
                  pl.BlockSpec((1, 1, d), lambda i, be, nu: (be[i], 0, 0))],
        out_specs=pl.BlockSpec((MOE_ROWS, d), lambda i, be, nu: (i, 0)),
        scratch_shapes=[pltpu.VMEM((d, ff2), BF16), pltpu.VMEM((d_ff, d), BF16)])
    return pl.pallas_call(
        functools.partial(_expert_kernel, d_ff=d_ff), grid_spec=grid_spec,
        out_shape=jax.ShapeDtypeStruct((rows, d), F32),
        compiler_params=_params("arbitrary"), name="moe_experts")(
            block_expert, n_used, xs, w_gate_up, b_gate_up.reshape(n_exp, 1, ff2),
            w_down, b_down.reshape(n_exp, 1, d))


def _combine_kernel(dest_ref, wt_ref, h_ref, lg_ref, lb_ref, y_ref, o_ref, ob_ref, rows, sem, *, tt, alpha):
    def copy(n, kk):
        return pltpu.make_async_copy(y_ref.at[pl.ds(dest_ref[n * TOP_K + kk], 1)], rows.at[kk, pl.ds(n, 1)], sem)

    def start(n, carry):
        for kk in range(TOP_K):
            copy(n, kk).start()
        return carry

    def wait(n, carry):
        for kk in range(TOP_K):
            copy(n, kk).wait()
        return carry

    lax.fori_loop(0, tt, start, 0)
    lax.fori_loop(0, tt, wait, 0)
    wt = wt_ref[...]
    moe = rows[0] * wt[:, 0:1]
    for kk in range(1, TOP_K):
        moe = moe + rows[kk] * wt[:, kk:kk + 1]
    y = _layer_norm(alpha * h_ref[...] + moe, lg_ref[...], lb_ref[...])
    o_ref[...] = y
    ob_ref[...] = y.astype(BF16)


def _combine(y, dest_flat, wt, h, ln_g, ln_b, alpha):
    m, d = h.shape
    tt = _tile(m, 128, 16)
    vec = pl.BlockSpec((1, d), lambda i: (0, 0))
    row = pl.BlockSpec((tt, d), lambda i: (i, 0))
    return pl.pallas_call(
        functools.partial(_combine_kernel, tt=tt, alpha=alpha), grid=(m // tt,),
        in_specs=[pl.BlockSpec((tt * TOP_K,), lambda i: (i,), memory_space=pltpu.SMEM),
                  pl.BlockSpec((tt, LANES), lambda i: (i, 0)), row, vec, vec,
                  pl.BlockSpec(memory_space=pl.ANY)],
        out_specs=[row, row],
        out_shape=[jax.ShapeDtypeStruct((m, d), F32), jax.ShapeDtypeStruct((m, d), BF16)],
        scratch_shapes=[pltpu.VMEM((TOP_K, tt, d), F32), pltpu.SemaphoreType.DMA(())],
        compiler_params=_params("arbitrary"), name="moe_combine")(dest_flat, wt, h, ln_g, ln_b, y)


def _moe(h, w, alpha, layer):
    m, d = h.shape
    n_exp = w["n_exp"]
    idx, wt, rank, counts = _router(h, w["w_router_pad"], w["b_router_pad"], n_exp)
    counts = counts[0, :n_exp]
    padded = (counts + MOE_ROWS - 1) // MOE_ROWS * MOE_ROWS
    p_end = jnp.cumsum(padded)
    p_start = p_end - padded
    dest = (p_start[idx[:, :TOP_K]] + rank[:, :TOP_K]).reshape(-1)
    n_blocks = -(-(m * TOP_K) // MOE_ROWS) + n_exp
    block_start = jnp.arange(n_blocks, dtype=jnp.int32) * MOE_ROWS
    block_expert = jnp.minimum(jnp.sum(p_end[None, :] <= block_start[:, None], axis=1), n_exp - 1).astype(jnp.int32)
    n_used = (p_end[-1:] // MOE_ROWS).astype(jnp.int32)
    xs = _dispatch(h, dest, p_end.astype(jnp.int32), n_blocks * MOE_ROWS)
    y = _experts(xs, block_expert, n_used, w["w_gate_up"], w["b_gate_up"], w["w_down"], w["b_down"], layer)
    return _combine(y, dest, wt, h, w["ln2_g"], w["ln2_b"], alpha)


def _mixers(h, hb, h1, w, dims, alpha, states, layer, row0):
    nb, t, d, gh, gdk, gdv, hh, hdk, hdv = dims
    z = _matmul(hb, w["w_in_main"], layer, row0, nb * t)
    la = _gla_decay(hb, w["w_lra"], w["w_gla_lr"], w["b_gla_lr"], row0, nb * t)
    col_conv = (2 * gh * gdk + 2 * gh * gdv) // d
    col_hgrn = col_conv + 5
    col_gate = col_hgrn + 4
    if states is None:
        oa, s_gla = _gla_prompt(z, la, w["gla_norm_g"], nb, t, gh, gdk, gdv)
        od, s_hgrn = _hgrn_prompt(z, w["hgrn_lb"], w["hgrn_norm_g"], nb, t, hh, hdk, hdv, col_hgrn)
        ob, oc, c_s, c_c = _conv_prompt(z, w["sconv_w"], w["conf_conv_w"], w["conf_conv_b"],
                                        w["conf_ln_g"], w["conf_ln_b"], nb, t, d, col_conv)
    else:
        st_gla, st_hgrn, new_gla, new_hgrn, cache_s, cache_c = states
        oa, s_gla = _gla_step(z, la, w["gla_norm_g"], st_gla, new_gla, layer)
        od, s_hgrn = _hgrn_step(z, w["hgrn_lb"], w["hgrn_norm_g"], st_hgrn, new_hgrn, layer, col_hgrn)
        ob, oc, u, uc = _conv_step(z, cache_s, cache_c, w["sconv_w"], w["conf_conv_w"], w["conf_conv_b"],
                                   w["conf_ln_g"], w["conf_ln_b"], d, col_conv)
        c_s = jnp.concatenate([cache_s[:, 1:], u[:, None]], axis=1)
        c_c = jnp.concatenate([cache_c[:, 1:], uc[:, None]], axis=1)
    h1 = _merge(oa, ob, oc, od, z, h, h1, w["w_br_a"], w["w_br_b"], w["w_br_c"], w["w_br_d"], w["w_o"],
                w["ln1_g"], w["ln1_b"], alpha, col_gate, row0)
    return h1, (s_gla, s_hgrn, c_s, c_c)


def kernel(x_prompt, x_sample, state_gla, state_hgrn, cache_sconv, cache_conformer, ln_in_g, ln_in_b, w_in, w_gla_lr, b_gla_lr, gla_norm_g, w_br_a, sconv_w, w_br_b, conf_conv_w, conf_conv_b, conf_ln_g, conf_ln_b, w_br_c, hgrn_lb_logits, hgrn_norm_g, w_br_d, w_o, ln1_g, ln1_b, w_router, b_router, w_gate_up, b_gate_up, w_down, b_down, ln2_g, ln2_b):
    depth, d, _ = w_in.shape
    _, _, gh, gdk, gdv = state_gla.shape
    _, _, hh, hdk, hdv = state_hgrn.shape
    rank = w_gla_lr.shape[1]
    n_exp = w_router.shape[2]
    alpha = (2 * depth) ** 0.25
    lra0 = 2 * gh * gdk + 2 * gh * gdv

    lb_p = jax.nn.softmax(hgrn_lb_logits.astype(F32), axis=0)
    hgrn_lb = jnp.cumsum(lb_p, axis=0) - lb_p[:1]
    row = lambda a: a.reshape(1, -1)
    w_in_main = jnp.concatenate([w_in[:, :, :lra0], w_in[:, :, lra0 + rank:]], axis=2).astype(BF16)
    layers = []
    for l in range(depth):
        w_lra = jnp.zeros((d, LANES), F32).at[:, :rank].set(w_in[l, :, lra0:lra0 + rank])
        w2 = jnp.zeros((LANES, gh * gdk), F32).at[:rank].set(w_gla_lr[l])
        layers.append({
            "w_in_main": w_in_main,
            "w_lra": w_lra.astype(BF16), "w_gla_lr": w2.astype(BF16), "b_gla_lr": row(b_gla_lr[l]),
            "gla_norm_g": row(gla_norm_g[l]), "hgrn_norm_g": row(hgrn_norm_g[l]), "hgrn_lb": row(hgrn_lb[l]),
            "sconv_w": sconv_w[l], "conf_conv_w": conf_conv_w[l], "conf_conv_b": row(conf_conv_b[l]),
            "conf_ln_g": row(conf_ln_g[l]), "conf_ln_b": row(conf_ln_b[l]),
            "w_br_a": w_br_a[l].astype(BF16), "w_br_b": w_br_b[l].astype(BF16),
            "w_br_c": w_br_c[l].astype(BF16), "w_br_d": w_br_d[l].astype(BF16), "w_o": w_o[l].astype(BF16),
            "ln1_g": row(ln1_g[l]), "ln1_b": row(ln1_b[l]), "ln2_g": row(ln2_g[l]), "ln2_b": row(ln2_b[l]),
            "n_exp": n_exp,
            "w_router_pad": jnp.zeros((d, LANES), F32).at[:, :n_exp].set(w_router[l]),
            "b_router_pad": jnp.zeros((1, LANES), F32).at[0, :n_exp].set(b_router[l]),
            "w_gate_up": w_gate_up, "b_gate_up": b_gate_up[l], "w_down": w_down, "b_down": b_down[l],
        })

    nbp, tp, _ = x_prompt.shape
    nbs, ts, _ = x_sample.shape
    assert ts == 1, "the sample group advances its states by exactly one token"
    dims_p = (nbp, tp, d, gh, gdk, gdv, hh, hdk, hdv)
    dims_s = (nbs, ts, d, gh, gdk, gdv, hh, hdk, hdv)
    mp, ms = nbp * tp, nbs * ts
    h, hb = _ln_call(jnp.concatenate([x_prompt.reshape(mp, d), x_sample.reshape(ms, d)], axis=0), ln_in_g, ln_in_b)
    prompt_states = ([], [], [], [])
    new_gla, new_hgrn = jnp.zeros_like(state_gla), jnp.zeros_like(state_hgrn)
    new_s, new_c = [], []
    for l, w in enumerate(layers):
        h1 = jnp.zeros_like(h)
        h1, st_p = _mixers(h, hb, h1, w, dims_p, alpha, None, l, 0)
        h1, (new_gla, new_hgrn, c_s, c_c) = _mixers(
            h, hb, h1, w, dims_s, alpha,
            (state_gla, state_hgrn, new_gla, new_hgrn, cache_sconv[l], cache_conformer[l]), l, mp)
        h, hb = _moe(h1, w, alpha, l)
        for acc, s in zip(prompt_states, st_p):
            acc.append(s)
        new_s.append(c_s)
        new_c.append(c_c)
    y_p, y_s = h[:mp].reshape(nbp, tp, d), h[mp:].reshape(nbs, ts, d)
    return ((y_p, y_s) + tuple(jnp.stack(a) for a in prompt_states)
            + (new_gla, new_hgrn, jnp.stack(new_s), jnp.stack(new_c)))
```

```python
import functools

import jax
import jax.numpy as jnp
from jax import lax
from jax.experimental import pallas as pl
from jax.experimental.pallas import tpu as pltpu

F32 = jnp.float32
BF16 = jnp.bfloat16

TOP_K = 4
CHUNK = 64
SUB = 16
GLA_GATE_NORM = 16.0
F_MIN = 1e-20
LN_EPS = 1e-5
RMS_EPS = 1e-6
SWIGLU_ALPHA = 1.702
SWIGLU_LIMIT = 7.0
LOG2_E = 1.4426950408889634
MOE_ROWS = 512
LANES = 128
SUBLANES = 8
CONV_ROWS = 32
CONF_HIST = 32
SCONV_HIST = 8
V7X_VMEM_LIMIT = 56 * 1024 * 1024


def _params(*sem):
    return pltpu.CompilerParams(dimension_semantics=sem, vmem_limit_bytes=V7X_VMEM_LIMIT)


def _tile(n, pref, mult=8):
    if n <= pref:
        return n
    t = pref - pref % mult
    while t >= mult:
        if n % t == 0:
            return t
        t -= mult
    raise ValueError(f"no tile for {n}")


def _sigmoid(x):
    return 1.0 / (1.0 + jnp.exp(-x))


def _silu(x):
    return x * _sigmoid(x)


def _log_sigmoid(x):
    return jnp.minimum(x, 0.0) - jnp.log(1.0 + jnp.exp(-jnp.abs(x)))


def _layer_norm(x, g, b):
    mu = jnp.mean(x, axis=-1, keepdims=True)
    xc = x - mu
    var = jnp.mean(xc * xc, axis=-1, keepdims=True)
    return xc * lax.rsqrt(var + LN_EPS) * g + b


def _dot(a, b):
    return jnp.dot(a, b, preferred_element_type=F32)


def _dot_nt(a, b):
    return lax.dot_general(a, b, (((1,), (1,)), ((), ())), preferred_element_type=F32)


def _dot_tn(a, b):
    return lax.dot_general(a, b, (((0,), (0,)), ((), ())), preferred_element_type=F32)


def _split3(x):
    hi = x.astype(BF16)
    r = x - hi.astype(F32)
    mid = r.astype(BF16)
    lo = (r - mid.astype(F32)).astype(BF16)
    return hi, mid, lo


def _ln_kernel(x_ref, g_ref, b_ref, o_ref, ob_ref):
    y = _layer_norm(x_ref[...], g_ref[...], b_ref[...])
    o_ref[...] = y
    ob_ref[...] = y.astype(BF16)


def _ln_call(x, g, b):
    m, d = x.shape
    tm = _tile(m, 512, 16)
    row = pl.BlockSpec((tm, d), lambda i: (i, 0))
    vec = pl.BlockSpec((1, d), lambda i: (0, 0))
    return pl.pallas_call(
        _ln_kernel, grid=(m // tm,), in_specs=[row, vec, vec], out_specs=[row, row],
        out_shape=[jax.ShapeDtypeStruct((m, d), F32), jax.ShapeDtypeStruct((m, d), BF16)],
        compiler_params=_params("parallel"), name="ln_in")(x, g.reshape(1, d), b.reshape(1, d))


def _mm_kernel(x_ref, w_ref, o_ref):
    o_ref[...] = _dot(x_ref[...], w_ref[0])


def _matmul(x, w, layer, row0, m):
    k = x.shape[1]
    n = w.shape[2]
    tm = _tile(m, 1024, 16)
    tn = _tile(n, 2048, LANES)
    assert row0 % tm == 0
    r0 = row0 // tm
    return pl.pallas_call(
        _mm_kernel, grid=(n // tn, m // tm),
        in_specs=[pl.BlockSpec((tm, k), lambda j, i: (r0 + i, 0)),
                  pl.BlockSpec((1, k, tn), lambda j, i: (layer, 0, j))],
        out_specs=pl.BlockSpec((tm, tn), lambda j, i: (i, j)),
        out_shape=jax.ShapeDtypeStruct((m, n), F32),
        compiler_params=_params("parallel", "parallel"), name="in_proj")(x, w)


def _gla_decay_kernel(h_ref, w1_ref, w2_ref, b2_ref, o_ref):
    lra = _dot(h_ref[...], w1_ref[...])
    x = _dot(lra.astype(BF16), w2_ref[...]) + b2_ref[...]
    o_ref[...] = _log_sigmoid(x) * (1.0 / GLA_GATE_NORM)


def _gla_decay(hb, w1, w2, b2, row0, m):
    d = hb.shape[1]
    n = w2.shape[1]
    tm = _tile(m, 512, 16)
    assert row0 % tm == 0
    r0 = row0 // tm
    return pl.pallas_call(
        _gla_decay_kernel, grid=(m // tm,),
        in_specs=[pl.BlockSpec((tm, d), lambda i: (r0 + i, 0)), pl.BlockSpec(w1.shape, lambda i: (0, 0)),
                  pl.BlockSpec(w2.shape, lambda i: (0, 0)), pl.BlockSpec((1, n), lambda i: (0, 0))],
        out_specs=pl.BlockSpec((tm, n), lambda i: (i, 0)),
        out_shape=jax.ShapeDtypeStruct((m, n), F32),
        compiler_params=_params("parallel"), name="gla_decay")(hb, w1, w2, b2)


def _cumsum_rows(x):
    c = x.shape[0]
    r = lax.broadcasted_iota(jnp.int32, (c, c), 0)
    s = lax.broadcasted_iota(jnp.int32, (c, c), 1)
    tri = jnp.where(r >= s, 1.0, 0.0).astype(BF16)
    hi, mid, lo = _split3(x)
    return _dot(tri, hi) + _dot(tri, mid) + _dot(tri, lo)


def _intra_scores(q, k, b2):
    c, dk = q.shape
    half = SUB // 2
    lane = lax.broadcasted_iota(jnp.int32, (half, c), 1)
    row = lax.broadcasted_iota(jnp.int32, (half, c), 0)
    blocks = []
    for i in range(c // SUB):
        lo = i * SUB
        qi, ki, bi = q[lo:lo + SUB], k[lo:lo + SUB], b2[lo:lo + SUB]
        if i > 0:
            ref = b2[lo - 1:lo]
            qs = qi * jnp.exp2(bi - ref)
            ks = jnp.concatenate([k[:lo] * jnp.exp2(ref - b2[:lo]), jnp.zeros((c - lo, dk), F32)], axis=0)
            a = _dot_nt(qs.astype(BF16), ks.astype(BF16))
            top, bot = a[:half], a[half:]
        else:
            top = bot = jnp.zeros((half, c), F32)
        for s in range(SUB):
            ks_row, bs_row = ki[s:s + 1], bi[s:s + 1]
            if s < half:
                p = qi[:half] * ks_row * jnp.exp2(bi[:half] - bs_row)
                top = jnp.where((lane == lo + s) & (row >= s), jnp.sum(p, axis=-1, keepdims=True), top)
            p = qi[half:] * ks_row * jnp.exp2(bi[half:] - bs_row)
            keep = (lane == lo + s) if s < half else (lane == lo + s) & (row >= s - half)
            bot = jnp.where(keep, jnp.sum(p, axis=-1, keepdims=True), bot)
        blocks += [top, bot]
    return jnp.concatenate(blocks, axis=0)


def _chunk_step(q, k, v, b, st):
    c = q.shape[0]
    b2 = b * LOG2_E
    b_end = b2[c - 1:c]
    o = _dot_nt((q * jnp.exp2(b2)).astype(BF16), st.astype(BF16))
    a = _intra_scores(q, k, b2)
    vb = v.astype(BF16)
    o = o + _dot(a.astype(BF16), vb)
    kb = (k * jnp.exp2(b_end - b2)).astype(BF16)
    st_new = st * jnp.exp2(b_end) + _dot_tn(vb, kb)
    return o, st_new


def _gated_rms(o, gate, g):
    o = o * lax.rsqrt(jnp.mean(o * o, axis=-1, keepdims=True) + RMS_EPS) * g
    return o * _silu(gate)


def _gla_kernel(q_ref, k_ref, v_ref, g_ref, la_ref, ng_ref, o_ref, so_ref, st_ref, *, heads, dk, dv):
    j = pl.program_id(1)

    @pl.when(j == 0)
    def _():
        st_ref[...] = jnp.zeros_like(st_ref)

    b_all = _cumsum_rows(la_ref[...])
    scale = dk ** -0.5
    for h in range(heads):
        ks = slice(h * dk, (h + 1) * dk)
        vs = slice(h * dv, (h + 1) * dv)
        o, st = _chunk_step(q_ref[:, ks] * scale, k_ref[:, ks], v_ref[:, vs], b_all[:, ks], st_ref[h])
        st_ref[h] = st
        o_ref[:, vs] = _gated_rms(o, g_ref[:, vs], ng_ref[...]).astype(BF16)

    @pl.when(j == pl.num_programs(1) - 1)
    def _():
        for h in range(heads):
            so_ref[0, h] = st_ref[h].T


def _gla_prompt(z, la, norm_g, nb, t, heads, dk, dv):
    c = min(CHUNK, t)
    n = t // c
    wk, wv = heads * dk, heads * dv
    row = lambda col: (lambda bi, j: (bi * n + j, col))
    kern = functools.partial(_gla_kernel, heads=heads, dk=dk, dv=dv)
    return pl.pallas_call(
        kern, grid=(nb, n),
        in_specs=[pl.BlockSpec((c, wk), row(0)), pl.BlockSpec((c, wk), row(1)),
                  pl.BlockSpec((c, wv), row(1)), pl.BlockSpec((c, wv), row(2)),
                  pl.BlockSpec((c, wk), row(0)), pl.BlockSpec((1, dv), lambda bi, j: (0, 0))],
        out_specs=[pl.BlockSpec((c, wv), row(0)),
                   pl.BlockSpec((1, heads, dk, dv), lambda bi, j: (bi, 0, 0, 0))],
        out_shape=[jax.ShapeDtypeStruct((nb * t, wv), BF16),
                   jax.ShapeDtypeStruct((nb, heads, dk, dv), F32)],
        scratch_shapes=[pltpu.VMEM((heads, dv, dk), F32)],
        compiler_params=_params("parallel", "arbitrary"), name="gla_prompt")(z, z, z, z, la, norm_g)


def _hgrn_inputs(qd, fd, lb):
    f = lb + (1.0 - lb) * _sigmoid(fd)
    return _silu(qd), 1.0 - f, jnp.log(jnp.maximum(f, F_MIN))


def _hgrn_kernel(q_ref, f_ref, v_ref, g_ref, lb_ref, ng_ref, o_ref, so_ref, st_ref, *, heads, dk, dv):
    j = pl.program_id(1)

    @pl.when(j == 0)
    def _():
        st_ref[...] = jnp.zeros_like(st_ref)

    q_all, k_all, lf = _hgrn_inputs(q_ref[...], f_ref[...], lb_ref[...])
    b_all = _cumsum_rows(lf)
    for h in range(heads):
        ks = slice(h * dk, (h + 1) * dk)
        vs = slice(h * dv, (h + 1) * dv)
        o, st = _chunk_step(q_all[:, ks], k_all[:, ks], v_ref[:, vs], b_all[:, ks], st_ref[h])
        st_ref[h] = st
        o_ref[:, vs] = _gated_rms(o, g_ref[:, vs], ng_ref[...]).astype(BF16)

    @pl.when(j == pl.num_programs(1) - 1)
    def _():
        for h in range(heads):
            so_ref[0, h] = st_ref[h].T


def _hgrn_prompt(z, lb, norm_g, nb, t, heads, dk, dv, col0):
    c = min(CHUNK, t)
    n = t // c
    w = heads * dk
    row = lambda col: (lambda bi, j: (bi * n + j, col0 + col))
    kern = functools.partial(_hgrn_kernel, heads=heads, dk=dk, dv=dv)
    return pl.pallas_call(
        kern, grid=(nb, n),
        in_specs=[pl.BlockSpec((c, w), row(0)), pl.BlockSpec((c, w), row(1)),
                  pl.BlockSpec((c, w), row(2)), pl.BlockSpec((c, w), row(3)),
                  pl.BlockSpec((1, w), lambda bi, j: (0, 0)), pl.BlockSpec((1, dv), lambda bi, j: (0, 0))],
        out_specs=[pl.BlockSpec((c, w), lambda bi, j: (bi * n + j, 0)),
                   pl.BlockSpec((1, heads, dk, dv), lambda bi, j: (bi, 0, 0, 0))],
        out_shape=[jax.ShapeDtypeStruct((nb * t, w), BF16),
                   jax.ShapeDtypeStruct((nb, heads, dk, dv), F32)],
        scratch_shapes=[pltpu.VMEM((heads, dv, dk), F32)],
        compiler_params=_params("parallel", "arbitrary"), name="hgrn_prompt")(z, z, z, z, lb, norm_g)


def _columns(x, n):
    w = x.shape[1]
    pad = jnp.concatenate([x, jnp.zeros((w - n, w), F32)], axis=0) if n < w else x
    return pad.T


def _step_heads(q_all, k_all, a_all, v_ref, g_ref, ng_ref, s_ref, o_ref, so_ref, *, heads, dk, dv, tb):
    for h in range(heads):
        ks = slice(h * dk, (h + 1) * dk)
        vs = slice(h * dv, (h + 1) * dv)
        qc, kc, ac = _columns(q_all[:, ks], tb), _columns(k_all[:, ks], tb), _columns(a_all[:, ks], tb)
        v = v_ref[:, vs]
        row = lax.broadcasted_iota(jnp.int32, (tb, dv), 0)
        o = jnp.zeros((tb, dv), F32)
        for n in range(tb):
            s_new = ac[:, n:n + 1] * s_ref[0, n, h] + kc[:, n:n + 1] * v[n:n + 1]
            so_ref[0, n, h] = s_new
            o = jnp.where(row == n, jnp.sum(qc[:, n:n + 1] * s_new, axis=0, keepdims=True), o)
        o_ref[:, vs] = _gated_rms(o, g_ref[:, vs], ng_ref[...]).astype(BF16)


def _gla_step_kernel(q_ref, k_ref, v_ref, g_ref, la_ref, ng_ref, s_ref, prev_ref, o_ref, so_ref,
                     *, heads, dk, dv, tb):
    del prev_ref
    _step_heads(q_ref[...] * dk ** -0.5, k_ref[...], jnp.exp(la_ref[...]), v_ref, g_ref, ng_ref, s_ref,
                o_ref, so_ref, heads=heads, dk=dk, dv=dv, tb=tb)


def _hgrn_step_kernel(q_ref, f_ref, v_ref, g_ref, lb_ref, ng_ref, s_ref, prev_ref, o_ref, so_ref,
                      *, heads, dk, dv, tb):
    del prev_ref
    q_all, k_all, lf = _hgrn_inputs(q_ref[...], f_ref[...], lb_ref[...])
    _step_heads(q_all, k_all, jnp.exp(lf), v_ref, g_ref, ng_ref, s_ref, o_ref, so_ref,
                heads=heads, dk=dk, dv=dv, tb=tb)


def _gla_step(z, la, norm_g, states, new_states, layer):
    m = z.shape[0]
    _, _, heads, dk, dv = states.shape
    tb = 8
    wk, wv = heads * dk, heads * dv
    row = lambda col: (lambda i: (i, col))
    st = pl.BlockSpec((1, tb, heads, dk, dv), lambda i: (layer, i, 0, 0, 0))
    kern = functools.partial(_gla_step_kernel, heads=heads, dk=dk, dv=dv, tb=tb)
    return pl.pallas_call(
        kern, grid=(m // tb,),
        in_specs=[pl.BlockSpec((tb, wk), row(0)), pl.BlockSpec((tb, wk), row(1)),
                  pl.BlockSpec((tb, wv), row(1)), pl.BlockSpec((tb, wv), row(2)),
                  pl.BlockSpec((tb, wk), row(0)), pl.BlockSpec((1, dv), lambda i: (0, 0)), st,
                  pl.BlockSpec(memory_space=pl.ANY)],
        out_specs=[pl.BlockSpec((tb, wv), row(0)), st],
        out_shape=[jax.ShapeDtypeStruct((m, wv), BF16), jax.ShapeDtypeStruct(states.shape, F32)],
        input_output_aliases={7: 1},
        compiler_params=_params("parallel"), name="gla_step")(z, z, z, z, la, norm_g, states, new_states)


def _hgrn_step(z, lb, norm_g, states, new_states, layer, col0):
    m = z.shape[0]
    _, _, heads, dk, dv = states.shape
    tb = 8
    w = heads * dk
    row = lambda col: (lambda i: (i, col0 + col))
    st = pl.BlockSpec((1, tb, heads, dk, dv), lambda i: (layer, i, 0, 0, 0))
    kern = functools.partial(_hgrn_step_kernel, heads=heads, dk=dk, dv=dv, tb=tb)
    return pl.pallas_call(
        kern, grid=(m // tb,),
        in_specs=[pl.BlockSpec((tb, w), row(0)), pl.BlockSpec((tb, w), row(1)),
                  pl.BlockSpec((tb, w), row(2)), pl.BlockSpec((tb, w), row(3)),
                  pl.BlockSpec((1, w), lambda i: (0, 0)), pl.BlockSpec((1, dv), lambda i: (0, 0)), st,
                  pl.BlockSpec(memory_space=pl.ANY)],
        out_specs=[pl.BlockSpec((tb, w), lambda i: (i, 0)), st],
        out_shape=[jax.ShapeDtypeStruct((m, w), BF16), jax.ShapeDtypeStruct(states.shape, F32)],
        input_output_aliases={7: 1},
        compiler_params=_params("parallel"), name="hgrn_step")(z, z, z, z, lb, norm_g, states, new_states)


def _taps(win, w_ref, lanes, first, n_taps):
    n = win.shape[0]
    acc = None
    for res in range(SUBLANES):
        offs = [o for o in range(first, first + n_taps) if o % SUBLANES == res]
        if not offs:
            continue
        sh = win if res == 0 else pltpu.roll(win, n - res, axis=0)
        for o in offs:
            base = o - res
            term = w_ref[o - first:o - first + 1, lanes] * sh[base:base + CONV_ROWS]
            acc = term if acc is None else acc + term
    return acc


def _conv_kernel(gb_ref, gc_ref, hb_ref, ga_ref, gs_ref, sw_ref, cw_ref, cb_ref, lg_ref, lb_ref,
                 ob_ref, oc_ref, so_ref, co_ref, ubuf, cbuf, *, tt, sw, cw):
    t = pl.program_id(1)

    @pl.when(t == 0)
    def _():
        ubuf[0:SCONV_HIST] = jnp.zeros((SCONV_HIST, ubuf.shape[1]), F32)
        cbuf[0:CONF_HIST] = jnp.zeros((CONF_HIST, cbuf.shape[1]), F32)

    ubuf[SCONV_HIST:SCONV_HIST + tt] = gc_ref[...] * hb_ref[...]
    cbuf[CONF_HIST:CONF_HIST + tt] = ga_ref[...] * _sigmoid(gs_ref[...])
    s0 = SCONV_HIST - (sw - 1)
    c0 = CONF_HIST - (cw - 1)

    def strip(r, carry):
        r0 = pl.multiple_of(r * CONV_ROWS, CONV_ROWS)
        yb, yc = [], []
        for lb in range(ubuf.shape[1] // LANES):
            ls = slice(lb * LANES, (lb + 1) * LANES)
            yb.append(_taps(ubuf[pl.ds(r0, CONV_ROWS + SCONV_HIST), ls], sw_ref, ls, s0, sw))
            yc.append(_taps(cbuf[pl.ds(r0, CONV_ROWS + CONF_HIST), ls], cw_ref, ls, c0, cw))
        acc = jnp.concatenate(yb, axis=1)
        ob_ref[pl.ds(r0, CONV_ROWS)] = (gb_ref[pl.ds(r0, CONV_ROWS)] * acc).astype(BF16)
        acc = jnp.concatenate(yc, axis=1)
        y = _layer_norm(acc + cb_ref[...], lg_ref[...], lb_ref[...])
        oc_ref[pl.ds(r0, CONV_ROWS)] = _silu(y).astype(BF16)
        return carry

    lax.fori_loop(0, tt // CONV_ROWS, strip, 0)

    @pl.when(t == pl.num_programs(1) - 1)
    def _():
        so_ref[0] = ubuf[SCONV_HIST + tt - (sw - 1):SCONV_HIST + tt]
        co_ref[0] = cbuf[CONF_HIST + tt - (cw - 1):CONF_HIST + tt]

    ubuf[0:SCONV_HIST] = ubuf[tt:tt + SCONV_HIST]
    cbuf[0:CONF_HIST] = cbuf[tt:tt + CONF_HIST]


def _conv_prompt(z, sconv_w, conf_w, conf_b, ln_g, ln_b, nb, t, d, col0):
    sw, cw = sconv_w.shape[0], conf_w.shape[0]
    tt = _tile(t, 256, CONF_HIST)
    n = t // tt
    row = lambda col: (lambda bi, j: (bi * n + j, col0 + col))
    vec = lambda r: pl.BlockSpec((r, d), lambda bi, j: (0, 0))
    kern = functools.partial(_conv_kernel, tt=tt, sw=sw, cw=cw)
    blk = lambda col: pl.BlockSpec((tt, d), row(col))
    return pl.pallas_call(
        kern, grid=(nb, n),
        in_specs=[blk(0), blk(1), blk(2), blk(3), blk(4), vec(sw), vec(cw), vec(1), vec(1), vec(1)],
        out_specs=[pl.BlockSpec((tt, d), lambda bi, j: (bi * n + j, 0)),
                   pl.BlockSpec((tt, d), lambda bi, j: (bi * n + j, 0)),
                   pl.BlockSpec((1, sw - 1, d), lambda bi, j: (bi, 0, 0)),
                   pl.BlockSpec((1, cw - 1, d), lambda bi, j: (bi, 0, 0))],
        out_shape=[jax.ShapeDtypeStruct((nb * t, d), BF16), jax.ShapeDtypeStruct((nb * t, d), BF16),
                   jax.ShapeDtypeStruct((nb, sw - 1, d), F32), jax.ShapeDtypeStruct((nb, cw - 1, d), F32)],
        scratch_shapes=[pltpu.VMEM((SCONV_HIST + tt, d), F32), pltpu.VMEM((CONF_HIST + tt, d), F32)],
        compiler_params=_params("parallel", "arbitrary"), name="conv_prompt")(
            z, z, z, z, z, sconv_w, conf_w, conf_b, ln_g, ln_b)


def _conv_step_kernel(gb_ref, gc_ref, hb_ref, ga_ref, gs_ref, sc_ref, cc_ref, sw_ref, cw_ref, cb_ref,
                      lg_ref, lb_ref, ob_ref, oc_ref, u_ref, uc_ref, *, tb, sw, cw):
    u = gc_ref[...] * hb_ref[...]
    uc = ga_ref[...] * _sigmoid(gs_ref[...])
    u_ref[...] = u
    uc_ref[...] = uc
    row = lax.broadcasted_iota(jnp.int32, u.shape, 0)
    ys = jnp.zeros_like(u)
    yc = jnp.zeros_like(u)
    for n in range(tb):
        ys = jnp.where(row == n, jnp.sum(sc_ref[n] * sw_ref[0:sw - 1], axis=0, keepdims=True), ys)
        yc = jnp.where(row == n, jnp.sum(cc_ref[n] * cw_ref[0:cw - 1], axis=0, keepdims=True), yc)
    yb = ys + sw_ref[sw - 1:sw] * u
    ob_ref[...] = (gb_ref[...] * yb).astype(BF16)
    y = yc + cw_ref[cw - 1:cw] * uc + cb_ref[...]
    oc_ref[...] = _silu(_layer_norm(y, lg_ref[...], lb_ref[...])).astype(BF16)


def _conv_step(z, cache_s, cache_c, sconv_w, conf_w, conf_b, ln_g, ln_b, d, col0):
    m = z.shape[0]
    sw, cw = sconv_w.shape[0], conf_w.shape[0]
    tb = 16
    blk = lambda col: pl.BlockSpec((tb, d), lambda i: (i, col0 + col))
    vec = lambda r: pl.BlockSpec((r, d), lambda i: (0, 0))
    out = pl.BlockSpec((tb, d), lambda i: (i, 0))
    kern = functools.partial(_conv_step_kernel, tb=tb, sw=sw, cw=cw)
    return pl.pallas_call(
        kern, grid=(m // tb,),
        in_specs=[blk(0), blk(1), blk(2), blk(3), blk(4),
                  pl.BlockSpec((tb, sw - 1, d), lambda i: (i, 0, 0)),
                  pl.BlockSpec((tb, cw - 1, d), lambda i: (i, 0, 0)),
                  vec(sw), vec(cw), vec(1), vec(1), vec(1)],
        out_specs=[out, out, out, out],
        out_shape=[jax.ShapeDtypeStruct((m, d), BF16), jax.ShapeDtypeStruct((m, d), BF16),
                   jax.ShapeDtypeStruct((m, d), F32), jax.ShapeDtypeStruct((m, d), F32)],
        compiler_params=_params("parallel"), name="conv_step")(
            z, z, z, z, z, cache_s, cache_c, sconv_w, conf_w, conf_b, ln_g, ln_b)


def _merge_kernel(oa_ref, ob_ref, oc_ref, od_ref, g0_ref, g1_ref, g2_ref, g3_ref, h_ref,
                  wa_ref, wb_ref, wc_ref, wd_ref, wo_ref, lg_ref, lb_ref, prev_ref, o_ref, *, alpha):
    del prev_ref
    merged = _sigmoid(g0_ref[...]) * _dot(oa_ref[...], wa_ref[...])
    merged = merged + _sigmoid(g1_ref[...]) * _dot(ob_ref[...], wb_ref[...])
    merged = merged + _sigmoid(g2_ref[...]) * _dot(oc_ref[...], wc_ref[...])
    merged = merged + _sigmoid(g3_ref[...]) * _dot(od_ref[...], wd_ref[...])
    out = _dot(merged.astype(BF16), wo_ref[...])
    o_ref[...] = _layer_norm(alpha * h_ref[...] + out, lg_ref[...], lb_ref[...])


def _merge(oa, ob, oc, od, z, h, out, wa, wb, wc, wd, wo, ln_g, ln_b, alpha, gate_col0, row0):
    m, d = oa.shape
    tm = _tile(m, 256, 16)
    assert row0 % tm == 0
    r0 = row0 // tm
    row = pl.BlockSpec((tm, d), lambda i: (i, 0))
    mrow = pl.BlockSpec((tm, d), lambda i: (r0 + i, 0))
    gate = lambda c: pl.BlockSpec((tm, d), lambda i: (i, gate_col0 + c))
    wsp = pl.BlockSpec((d, d), lambda i: (0, 0))
    vec = pl.BlockSpec((1, d), lambda i: (0, 0))
    return pl.pallas_call(
        functools.partial(_merge_kernel, alpha=alpha), grid=(m // tm,),
        in_specs=[row, row, row, row, gate(0), gate(1), gate(2), gate(3), mrow, wsp, wsp, wsp, wsp, wsp, vec, vec,
                  pl.BlockSpec(memory_space=pl.ANY)],
        out_specs=mrow, out_shape=jax.ShapeDtypeStruct(out.shape, F32),
        input_output_aliases={16: 0},
        compiler_params=_params("parallel"), name="merge")(
            oa, ob, oc, od, z, z, z, z, h, wa, wb, wc, wd, wo, ln_g, ln_b, out)


def _router_kernel(x_ref, w_ref, b_ref, wt_ref, pos_ref, len_ref, off_ref, base_ref, tot_ref, carry_ref,
                   *, n_exp):
    i = pl.program_id(0)

    @pl.when(i == 0)
    def _():
        carry_ref[...] = jnp.zeros_like(carry_ref)

    tm = x_ref.shape[0]
    logits = _dot(x_ref[...].astype(BF16), w_ref[...].astype(BF16)) + b_ref[...]
    lane = lax.broadcasted_iota(jnp.int32, (tm, LANES), 1)
    lane_f = lane.astype(F32)
    work = jnp.where(lane < n_exp, logits, -jnp.inf)
    vals, hots = [], []
    for kk in range(TOP_K):
        m = jnp.max(work, axis=-1, keepdims=True)
        first = jnp.min(jnp.where(work == m, lane_f, float(LANES)), axis=-1, keepdims=True)
        hot = lane_f == first
        work = jnp.where(hot, -jnp.inf, work)
        vals.append(m)
        hots.append(hot)
    es = [jnp.exp(v - vals[0]) for v in vals]
    den = es[0]
    for e in es[1:]:
        den = den + e
    wt = jnp.zeros((tm, LANES), F32)
    for kk in range(TOP_K):
        wt = jnp.where(lane == kk, es[kk] / den, wt)
    chosen = hots[0]
    for hot in hots[1:]:
        chosen = chosen | hot
    onehot = jnp.where(chosen, 1.0, 0.0)
    r = lax.broadcasted_iota(jnp.int32, (tm, tm), 0)
    c = lax.broadcasted_iota(jnp.int32, (tm, tm), 1)
    before = jnp.where(c < r, 1.0, 0.0).astype(BF16)
    seen = _dot(before, onehot.astype(BF16))
    cnt = jnp.sum(onehot, axis=0, keepdims=True)
    groups = jnp.floor((cnt + (SUBLANES - 1)) * (1.0 / SUBLANES))
    er = lax.broadcasted_iota(jnp.int32, (LANES, LANES), 0)
    ec = lax.broadcasted_iota(jnp.int32, (LANES, LANES), 1)
    earlier = jnp.where(er < ec, 1.0, 0.0).astype(BF16)
    groups8 = jnp.broadcast_to(groups, (SUBLANES, LANES)).astype(BF16)
    toff = _dot(groups8, earlier)[0:1] * float(SUBLANES)
    cnt_pad = groups * float(SUBLANES)
    where_row = toff + seen
    pos = jnp.zeros((tm, LANES), F32)
    for kk in range(TOP_K):
        pk = jnp.sum(jnp.where(hots[kk], where_row, 0.0), axis=-1, keepdims=True)
        pos = jnp.where(lane == kk, pk, pos)
    wt_ref[...] = wt
    pos_ref[...] = pos.astype(jnp.int32)
    len_ref[0] = cnt_pad.astype(jnp.int32)
    off_ref[0] = toff.astype(jnp.int32)
    base_ref[0] = carry_ref[...].astype(jnp.int32)
    carry_ref[...] = carry_ref[...] + cnt_pad
    tot_ref[...] = carry_ref[...].astype(jnp.int32)


def _router(x, w_pad, b_pad, n_exp, tm):
    m, d = x.shape
    row = pl.BlockSpec((tm, LANES), lambda i: (i, 0))
    one = pl.BlockSpec((1, LANES), lambda i: (0, 0))
    meta = pl.BlockSpec((1, 1, LANES), lambda i: (i, 0, 0))
    n = m // tm
    meta_shape = jax.ShapeDtypeStruct((n, 1, LANES), jnp.int32)
    return pl.pallas_call(
        functools.partial(_router_kernel, n_exp=n_exp), grid=(n,),
        in_specs=[pl.BlockSpec((tm, d), lambda i: (i, 0)), pl.BlockSpec((d, LANES), lambda i: (0, 0)), one],
        out_specs=[row, row, meta, meta, meta, one],
        out_shape=[jax.ShapeDtypeStruct((m, LANES), F32), jax.ShapeDtypeStruct((m, LANES), jnp.int32),
                   meta_shape, meta_shape, meta_shape, jax.ShapeDtypeStruct((1, LANES), jnp.int32)],
        scratch_shapes=[pltpu.VMEM((1, LANES), F32)],
        compiler_params=_params("arbitrary"), name="router")(x, w_pad, b_pad)


def _dispatch_kernel(pend_ref, len_ref, off_ref, dst_ref, x_ref, pos_ref, xs_ref, zbuf, grouped, sem, zsem,
                     *, n_exp, sizes):
    @pl.when(pl.program_id(0) == 0)
    def _():
        zbuf[...] = jnp.zeros_like(zbuf)

        def zero_copy(e):
            first = pl.multiple_of(pend_ref[e] - MOE_ROWS, MOE_ROWS)
            return pltpu.make_async_copy(zbuf, xs_ref.at[pl.ds(first, MOE_ROWS)], zsem)

        def nonempty(e):
            return pend_ref[e] > jnp.where(e > 0, pend_ref[jnp.maximum(e - 1, 0)], 0)

        def zstart(e, carry):
            @pl.when(nonempty(e))
            def _():
                zero_copy(e).start()
            return carry

        def zwait(e, carry):
            @pl.when(nonempty(e))
            def _():
                zero_copy(e).wait()
            return carry

        lax.fori_loop(0, n_exp, zstart, 0)
        lax.fori_loop(0, n_exp, zwait, 0)

        def tail_copy(b):
            return pltpu.make_async_copy(zbuf, xs_ref.at[pl.ds(pl.multiple_of(b * MOE_ROWS, MOE_ROWS), MOE_ROWS)], zsem)

        def tstart(b, carry):
            tail_copy(b).start()
            return carry

        def twait(b, carry):
            tail_copy(b).wait()
            return carry

        first_unused = pend_ref[n_exp - 1] // MOE_ROWS
        lax.fori_loop(first_unused, xs_ref.shape[0] // MOE_ROWS, tstart, 0)
        lax.fori_loop(first_unused, xs_ref.shape[0] // MOE_ROWS, twait, 0)

    tt = x_ref.shape[0]
    cap = grouped.shape[0]
    slot = lax.broadcasted_iota(jnp.int32, (tt, cap), 1)
    pos = pos_ref[...]
    placed = slot == pos[:, 0:1]
    for kk in range(1, TOP_K):
        placed = placed | (slot == pos[:, kk:kk + 1])
    grouped[...] = _dot_tn(jnp.where(placed, 1.0, 0.0).astype(BF16), x_ref[...].astype(BF16))

    def piece(e, offset, size):
        src = pl.multiple_of(off_ref[e] + offset, SUBLANES)
        dst = pl.multiple_of(dst_ref[e] + offset, SUBLANES)
        return pltpu.make_async_copy(grouped.at[pl.ds(src, size)], xs_ref.at[pl.ds(dst, size)], sem)

    _for_each_run_piece(len_ref, n_exp, sizes, lambda *a: piece(*a).start())
    _for_each_run_piece(len_ref, n_exp, sizes, lambda *a: piece(*a).wait())


def _run_sizes(tt):
    sizes = [SUBLANES]
    while sizes[-1] * 2 <= tt + SUBLANES - 1:
        sizes.append(sizes[-1] * 2)
    return tuple(reversed(sizes))


def _for_each_run_piece(len_ref, n_exp, sizes, fn):
    def body(e, carry):
        length = len_ref[e]
        for size in sizes:
            @pl.when((length & size) != 0)
            def _():
                fn(e, length & ~(2 * size - 1), size)
        return carry

    lax.fori_loop(0, n_exp, body, 0)


def _grouped_rows(tt, n_exp):
    cap = tt * TOP_K + n_exp * SUBLANES
    return -(-cap // LANES) * LANES


def _dispatch(x, pos, run_len, run_off, run_dst, p_end, rows, tt):
    m, d = x.shape
    n_exp = p_end.shape[0]
    meta = pl.BlockSpec((LANES,), lambda i, pe: (i,), memory_space=pltpu.SMEM)
    grid_spec = pltpu.PrefetchScalarGridSpec(
        num_scalar_prefetch=1, grid=(m // tt,),
        in_specs=[meta, meta, meta,
                  pl.BlockSpec((tt, d), lambda i, pe: (i, 0)),
                  pl.BlockSpec((tt, LANES), lambda i, pe: (i, 0))],
        out_specs=pl.BlockSpec(memory_space=pl.ANY),
        scratch_shapes=[pltpu.VMEM((MOE_ROWS, d), F32), pltpu.VMEM((_grouped_rows(tt, n_exp), d), F32),
                        pltpu.SemaphoreType.DMA(()), pltpu.SemaphoreType.DMA(())])
    return pl.pallas_call(
        functools.partial(_dispatch_kernel, n_exp=n_exp, sizes=_run_sizes(tt)), grid_spec=grid_spec,
        out_shape=jax.ShapeDtypeStruct((rows, d), F32),
        compiler_params=_params("arbitrary"), name="moe_dispatch")(p_end, run_len, run_off, run_dst, x, pos)


def _expert_kernel(be_ref, nu_ref, xs_ref, wgu_ref, bgu_ref, wd_ref, bd_ref, y_ref, wgu_bf, wd_bf, *, d_ff):
    i = pl.program_id(0)
    prev = be_ref[jnp.maximum(i - 1, 0)]

    @pl.when((i == 0) | (be_ref[i] != prev))
    def _():
        wgu_bf[...] = wgu_ref[0, 0].astype(BF16)
        wd_bf[...] = wd_ref[0, 0].astype(BF16)

    @pl.when(i < nu_ref[0])
    def _():
        gu = _dot(xs_ref[...].astype(BF16), wgu_bf[...]) + bgu_ref[0]
        x_glu = jnp.minimum(gu[:, :d_ff], SWIGLU_LIMIT)
        x_lin = jnp.clip(gu[:, d_ff:], -SWIGLU_LIMIT, SWIGLU_LIMIT)
        act = x_glu * _sigmoid(SWIGLU_ALPHA * x_glu) * (x_lin + 1.0)
        y_ref[...] = _dot(act.astype(BF16), wd_bf[...]) + bd_ref[0]

    @pl.when(i >= nu_ref[0])
    def _():
        y_ref[...] = jnp.zeros_like(y_ref)


def _experts(xs, block_expert, n_used, w_gate_up, b_gate_up, w_down, b_down, layer):
    rows, d = xs.shape
    _, n_exp, _, ff2 = w_gate_up.shape
    d_ff = ff2 // 2
    n_blocks = rows // MOE_ROWS
    blk = lambda i, be, nu: (jnp.minimum(i, nu[0] - 1), 0)
    grid_spec = pltpu.PrefetchScalarGridSpec(
        num_scalar_prefetch=2, grid=(n_blocks,),
        in_specs=[pl.BlockSpec((MOE_ROWS, d), blk),
                  pl.BlockSpec((1, 1, d, ff2), lambda i, be, nu: (layer, be[i], 0, 0)),
                  pl.BlockSpec((1, 1, ff2), lambda i, be, nu: (be[i], 0, 0)),
                  pl.BlockSpec((1, 1, d_ff, d), lambda i, be, nu: (layer, be[i], 0, 0)),
                  pl.BlockSpec((1, 1, d), lambda i, be, nu: (be[i], 0, 0))],
        out_specs=pl.BlockSpec((MOE_ROWS, d), lambda i, be, nu: (i, 0)),
        scratch_shapes=[pltpu.VMEM((d, ff2), BF16), pltpu.VMEM((d_ff, d), BF16)])
    return pl.pallas_call(
        functools.partial(_expert_kernel, d_ff=d_ff), grid_spec=grid_spec,
        out_shape=jax.ShapeDtypeStruct((rows, d), F32),
        compiler_params=_params("arbitrary"), name="moe_experts")(
            block_expert, n_used, xs, w_gate_up, b_gate_up.reshape(n_exp, 1, ff2),
            w_down, b_down.reshape(n_exp, 1, d))


def _combine_kernel(len_ref, off_ref, dst_ref, pos_ref, wt_ref, h_ref, lg_ref, lb_ref, y_ref, o_ref, ob_ref,
                    rows, moe, sem, *, n_exp, sizes, alpha):
    def piece(e, offset, size):
        src = pl.multiple_of(dst_ref[e] + offset, SUBLANES)
        dst = pl.multiple_of(off_ref[e] + offset, SUBLANES)
        return pltpu.make_async_copy(y_ref.at[pl.ds(src, size)], rows.at[pl.ds(dst, size)], sem)

    _for_each_run_piece(len_ref, n_exp, sizes, lambda *a: piece(*a).start())
    _for_each_run_piece(len_ref, n_exp, sizes, lambda *a: piece(*a).wait())

    def token(n, carry):
        acc = wt_ref[n * TOP_K] * rows[pl.ds(pos_ref[n * TOP_K], 1), :]
        for kk in range(1, TOP_K):
            acc = acc + wt_ref[n * TOP_K + kk] * rows[pl.ds(pos_ref[n * TOP_K + kk], 1), :]
        moe[pl.ds(n, 1), :] = acc
        return carry

    lax.fori_loop(0, h_ref.shape[0], token, 0)
    y = _layer_norm(alpha * h_ref[...] + moe[...], lg_ref[...], lb_ref[...])
    o_ref[...] = y
    ob_ref[...] = y.astype(BF16)


def _combine(y, pos_flat, wt_flat, run_len, run_off, run_dst, h, ln_g, ln_b, alpha, n_exp, tt):
    m, d = h.shape
    vec = pl.BlockSpec((1, d), lambda i: (0, 0))
    row = pl.BlockSpec((tt, d), lambda i: (i, 0))
    meta = pl.BlockSpec((LANES,), lambda i: (i,), memory_space=pltpu.SMEM)
    per_tok = pl.BlockSpec((pos_flat.shape[0] // (m // tt),), lambda i: (i,), memory_space=pltpu.SMEM)
    return pl.pallas_call(
        functools.partial(_combine_kernel, n_exp=n_exp, sizes=_run_sizes(tt), alpha=alpha), grid=(m // tt,),
        in_specs=[meta, meta, meta, per_tok, per_tok, row, vec, vec, pl.BlockSpec(memory_space=pl.ANY)],
        out_specs=[row, row],
        out_shape=[jax.ShapeDtypeStruct((m, d), F32), jax.ShapeDtypeStruct((m, d), BF16)],
        scratch_shapes=[pltpu.VMEM((_grouped_rows(tt, n_exp), d), F32), pltpu.VMEM((tt, d), F32),
                        pltpu.SemaphoreType.DMA(())],
        compiler_params=_params("arbitrary"), name="moe_combine")(
            run_len, run_off, run_dst, pos_flat, wt_flat, h, ln_g, ln_b, y)


def _moe(h, w, alpha, layer):
    m, d = h.shape
    n_exp = w["n_exp"]
    tt = _tile(m, 384, 16)
    n_tiles = m // tt
    wt, pos, run_len, run_off, run_base, totals = _router(h, w["w_router_pad"], w["b_router_pad"], n_exp, tt)
    seg = totals[0, :n_exp]
    padded = (seg + MOE_ROWS - 1) // MOE_ROWS * MOE_ROWS
    p_end = jnp.cumsum(padded).astype(jnp.int32)
    p_start = jnp.zeros((LANES,), jnp.int32).at[:n_exp].set(p_end - padded)
    run_dst = (run_base[:, 0, :] + p_start[None, :]).reshape(-1)
    run_len, run_off = run_len.reshape(-1), run_off.reshape(-1)
    n_blocks = -(-(m * TOP_K + n_tiles * n_exp * (SUBLANES - 1)) // MOE_ROWS) + n_exp
    block_start = jnp.arange(n_blocks, dtype=jnp.int32) * MOE_ROWS
    block_expert = jnp.minimum(jnp.sum(p_end[None, :] <= block_start[:, None], axis=1), n_exp - 1).astype(jnp.int32)
    n_used = p_end[-1:] // MOE_ROWS
    xs = _dispatch(h, pos, run_len, run_off, run_dst, p_end, n_blocks * MOE_ROWS, tt)
    y = _experts(xs, block_expert, n_used, w["w_gate_up"], w["b_gate_up"], w["w_down"], w["b_down"], layer)
    chunk = max(LANES, 1 << (tt * TOP_K - 1).bit_length())
    per_tile = lambda a: jnp.pad(a[:, :TOP_K].reshape(n_tiles, tt * TOP_K),
                                 ((0, 0), (0, chunk - tt * TOP_K))).reshape(-1)
    return _combine(y, per_tile(pos), per_tile(wt), run_len, run_off, run_dst,
                    h, w["ln2_g"], w["ln2_b"], alpha, n_exp, tt)


def _mixers(h, hb, h1, w, dims, alpha, states, layer, row0):
    nb, t, d, gh, gdk, gdv, hh, hdk, hdv = dims
    z = _matmul(hb, w["w_in_main"], layer, row0, nb * t)
    la = _gla_decay(hb, w["w_lra"], w["w_gla_lr"], w["b_gla_lr"], row0, nb * t)
    col_conv = (2 * gh * gdk + 2 * gh * gdv) // d
    col_hgrn = col_conv + 5
    col_gate = col_hgrn + 4
    if states is None:
        oa, s_gla = _gla_prompt(z, la, w["gla_norm_g"], nb, t, gh, gdk, gdv)
        od, s_hgrn = _hgrn_prompt(z, w["hgrn_lb"], w["hgrn_norm_g"], nb, t, hh, hdk, hdv, col_hgrn)
        ob, oc, c_s, c_c = _conv_prompt(z, w["sconv_w"], w["conf_conv_w"], w["conf_conv_b"],
                                        w["conf_ln_g"], w["conf_ln_b"], nb, t, d, col_conv)
    else:
        st_gla, st_hgrn, new_gla, new_hgrn, cache_s, cache_c = states
        oa, s_gla = _gla_step(z, la, w["gla_norm_g"], st_gla, new_gla, layer)
        od, s_hgrn = _hgrn_step(z, w["hgrn_lb"], w["hgrn_norm_g"], st_hgrn, new_hgrn, layer, col_hgrn)
        ob, oc, u, uc = _conv_step(z, cache_s, cache_c, w["sconv_w"], w["conf_conv_w"], w["conf_conv_b"],
                                   w["conf_ln_g"], w["conf_ln_b"], d, col_conv)
        c_s = jnp.concatenate([cache_s[:, 1:], u[:, None]], axis=1)
        c_c = jnp.concatenate([cache_c[:, 1:], uc[:, None]], axis=1)
    h1 = _merge(oa, ob, oc, od, z, h, h1, w["w_br_a"], w["w_br_b"], w["w_br_c"], w["w_br_d"], w["w_o"],
                w["ln1_g"], w["ln1_b"], alpha, col_gate, row0)
    return h1, (s_gla, s_hgrn, c_s, c_c)


def kernel(x_prompt, x_sample, state_gla, state_hgrn, cache_sconv, cache_conformer, ln_in_g, ln_in_b, w_in, w_gla_lr, b_gla_lr, gla_norm_g, w_br_a, sconv_w, w_br_b, conf_conv_w, conf_conv_b, conf_ln_g, conf_ln_b, w_br_c, hgrn_lb_logits, hgrn_norm_g, w_br_d, w_o, ln1_g, ln1_b, w_router, b_router, w_gate_up, b_gate_up, w_down, b_down, ln2_g, ln2_b):
    depth, d, _ = w_in.shape
    _, _, gh, gdk, gdv = state_gla.shape
    _, _, hh, hdk, hdv = state_hgrn.shape
    rank = w_gla_lr.shape[1]
    n_exp = w_router.shape[2]
    alpha = (2 * depth) ** 0.25
    lra0 = 2 * gh * gdk + 2 * gh * gdv

    lb_p = jax.nn.softmax(hgrn_lb_logits.astype(F32), axis=0)
    hgrn_lb = jnp.cumsum(lb_p, axis=0) - lb_p[:1]
    row = lambda a: a.reshape(1, -1)
    w_in_main = jnp.concatenate([w_in[:, :, :lra0], w_in[:, :, lra0 + rank:]], axis=2).astype(BF16)
    layers = []
    for l in range(depth):
        w_lra = jnp.zeros((d, LANES), F32).at[:, :rank].set(w_in[l, :, lra0:lra0 + rank])
        w2 = jnp.zeros((LANES, gh * gdk), F32).at[:rank].set(w_gla_lr[l])
        layers.append({
            "w_in_main": w_in_main,
            "w_lra": w_lra.astype(BF16), "w_gla_lr": w2.astype(BF16), "b_gla_lr": row(b_gla_lr[l]),
            "gla_norm_g": row(gla_norm_g[l]), "hgrn_norm_g": row(hgrn_norm_g[l]), "hgrn_lb": row(hgrn_lb[l]),
            "sconv_w": sconv_w[l], "conf_conv_w": conf_conv_w[l], "conf_conv_b": row(conf_conv_b[l]),
            "conf_ln_g": row(conf_ln_g[l]), "conf_ln_b": row(conf_ln_b[l]),
            "w_br_a": w_br_a[l].astype(BF16), "w_br_b": w_br_b[l].astype(BF16),
            "w_br_c": w_br_c[l].astype(BF16), "w_br_d": w_br_d[l].astype(BF16), "w_o": w_o[l].astype(BF16),
            "ln1_g": row(ln1_g[l]), "ln1_b": row(ln1_b[l]), "ln2_g": row(ln2_g[l]), "ln2_b": row(ln2_b[l]),
            "n_exp": n_exp,
            "w_router_pad": jnp.zeros((d, LANES), F32).at[:, :n_exp].set(w_router[l]),
            "b_router_pad": jnp.zeros((1, LANES), F32).at[0, :n_exp].set(b_router[l]),
            "w_gate_up": w_gate_up, "b_gate_up": b_gate_up[l], "w_down": w_down, "b_down": b_down[l],
        })

    nbp, tp, _ = x_prompt.shape
    nbs, ts, _ = x_sample.shape
    assert ts == 1, "the sample group advances its states by exactly one token"
    dims_p = (nbp, tp, d, gh, gdk, gdv, hh, hdk, hdv)
    dims_s = (nbs, ts, d, gh, gdk, gdv, hh, hdk, hdv)
    mp, ms = nbp * tp, nbs * ts
    h, hb = _ln_call(jnp.concatenate([x_prompt.reshape(mp, d), x_sample.reshape(ms, d)], axis=0), ln_in_g, ln_in_b)
    prompt_states = ([], [], [], [])
    new_gla, new_hgrn = jnp.zeros_like(state_gla), jnp.zeros_like(state_hgrn)
    new_s, new_c = [], []
    for l, w in enumerate(layers):
        h1 = jnp.zeros_like(h)
        h1, st_p = _mixers(h, hb, h1, w, dims_p, alpha, None, l, 0)
        h1, (new_gla, new_hgrn, c_s, c_c) = _mixers(
            h, hb, h1, w, dims_s, alpha,
            (state_gla, state_hgrn, new_gla, new_hgrn, cache_sconv[l], cache_conformer[l]), l, mp)
        h, hb = _moe(h1, w, alpha, l)
        for acc, s in zip(prompt_states, st_p):
            acc.append(s)
        new_s.append(c_s)
        new_c.append(c_c)
    y_p, y_s = h[:mp].reshape(nbp, tp, d), h[mp:].reshape(nbs, ts, d)
    return ((y_p, y_s) + tuple(jnp.stack(a) for a in prompt_states)
            + (new_gla, new_hgrn, jnp.stack(new_s), jnp.stack(new_c)))
```

```python
import functools

import jax
import jax.numpy as jnp
from jax import lax
from jax.experimental import pallas as pl
from jax.experimental.pallas import tpu as pltpu

F32 = jnp.float32
BF16 = jnp.bfloat16

TOP_K = 4
CHUNK = 64
SUB = 16
GLA_GATE_NORM = 16.0
F_MIN = 1e-20
LN_EPS = 1e-5
RMS_EPS = 1e-6
SWIGLU_ALPHA = 1.702
SWIGLU_LIMIT = 7.0
LOG2_E = 1.4426950408889634
MOE_ROWS = 512
LANES = 128
SUBLANES = 8
CONV_ROWS = 32
CONF_HIST = 32
SCONV_HIST = 8
V7X_VMEM_LIMIT = 56 * 1024 * 1024


def _params(*sem):
    return pltpu.CompilerParams(dimension_semantics=sem, vmem_limit_bytes=V7X_VMEM_LIMIT)


def _tile(n, pref, mult=8):
    if n <= pref:
        return n
    t = pref - pref % mult
    while t >= mult:
        if n % t == 0:
            return t
        t -= mult
    raise ValueError(f"no tile for {n}")


def _sigmoid(x):
    return 1.0 / (1.0 + jnp.exp(-x))


def _silu(x):
    return x * _sigmoid(x)


def _log_sigmoid(x):
    return jnp.minimum(x, 0.0) - jnp.log(1.0 + jnp.exp(-jnp.abs(x)))


def _layer_norm(x, g, b):
    mu = jnp.mean(x, axis=-1, keepdims=True)
    xc = x - mu
    var = jnp.mean(xc * xc, axis=-1, keepdims=True)
    return xc * lax.rsqrt(var + LN_EPS) * g + b


def _dot(a, b):
    return jnp.dot(a, b, preferred_element_type=F32)


def _dot_nt(a, b):
    return lax.dot_general(a, b, (((1,), (1,)), ((), ())), preferred_element_type=F32)


def _dot_tn(a, b):
    return lax.dot_general(a, b, (((0,), (0,)), ((), ())), preferred_element_type=F32)


def _split3(x):
    hi = x.astype(BF16)
    r = x - hi.astype(F32)
    mid = r.astype(BF16)
    lo = (r - mid.astype(F32)).astype(BF16)
    return hi, mid, lo


def _ln_kernel(x_ref, g_ref, b_ref, o_ref, ob_ref):
    y = _layer_norm(x_ref[...], g_ref[...], b_ref[...])
    o_ref[...] = y
    ob_ref[...] = y.astype(BF16)


def _ln_call(x, g, b):
    m, d = x.shape
    tm = _tile(m, 512, 16)
    row = pl.BlockSpec((tm, d), lambda i: (i, 0))
    vec = pl.BlockSpec((1, d), lambda i: (0, 0))
    return pl.pallas_call(
        _ln_kernel, grid=(m // tm,), in_specs=[row, vec, vec], out_specs=[row, row],
        out_shape=[jax.ShapeDtypeStruct((m, d), F32), jax.ShapeDtypeStruct((m, d), BF16)],
        compiler_params=_params("parallel"), name="ln_in")(x, g.reshape(1, d), b.reshape(1, d))


def _mm_kernel(x_ref, w_ref, o_ref):
    o_ref[...] = _dot(x_ref[...], w_ref[0])


def _matmul(x, w, layer, row0, m):
    k = x.shape[1]
    n = w.shape[2]
    tm = _tile(m, 1024, 16)
    tn = _tile(n, 2048, LANES)
    assert row0 % tm == 0
    r0 = row0 // tm
    return pl.pallas_call(
        _mm_kernel, grid=(n // tn, m // tm),
        in_specs=[pl.BlockSpec((tm, k), lambda j, i: (r0 + i, 0)),
                  pl.BlockSpec((1, k, tn), lambda j, i: (layer, 0, j))],
        out_specs=pl.BlockSpec((tm, tn), lambda j, i: (i, j)),
        out_shape=jax.ShapeDtypeStruct((m, n), F32),
        compiler_params=_params("parallel", "parallel"), name="in_proj")(x, w)


def _gla_decay_kernel(h_ref, w1_ref, w2_ref, b2_ref, o_ref):
    lra = _dot(h_ref[...], w1_ref[...])
    x = _dot(lra.astype(BF16), w2_ref[...]) + b2_ref[...]
    o_ref[...] = _log_sigmoid(x) * (1.0 / GLA_GATE_NORM)


def _gla_decay(hb, w1, w2, b2, row0, m):
    d = hb.shape[1]
    n = w2.shape[1]
    tm = _tile(m, 512, 16)
    assert row0 % tm == 0
    r0 = row0 // tm
    return pl.pallas_call(
        _gla_decay_kernel, grid=(m // tm,),
        in_specs=[pl.BlockSpec((tm, d), lambda i: (r0 + i, 0)), pl.BlockSpec(w1.shape, lambda i: (0, 0)),
                  pl.BlockSpec(w2.shape, lambda i: (0, 0)), pl.BlockSpec((1, n), lambda i: (0, 0))],
        out_specs=pl.BlockSpec((tm, n), lambda i: (i, 0)),
        out_shape=jax.ShapeDtypeStruct((m, n), F32),
        compiler_params=_params("parallel"), name="gla_decay")(hb, w1, w2, b2)


def _cumsum_rows(x):
    c = x.shape[0]
    r = lax.broadcasted_iota(jnp.int32, (c, c), 0)
    s = lax.broadcasted_iota(jnp.int32, (c, c), 1)
    tri = jnp.where(r >= s, 1.0, 0.0).astype(BF16)
    hi, mid, lo = _split3(x)
    return _dot(tri, hi) + _dot(tri, mid) + _dot(tri, lo)


def _intra_scores(q, k, b2):
    c, dk = q.shape
    half = SUB // 2
    lane = lax.broadcasted_iota(jnp.int32, (half, c), 1)
    row = lax.broadcasted_iota(jnp.int32, (half, c), 0)
    blocks = []
    for i in range(c // SUB):
        lo = i * SUB
        qi, ki, bi = q[lo:lo + SUB], k[lo:lo + SUB], b2[lo:lo + SUB]
        if i > 0:
            ref = b2[lo - 1:lo]
            qs = qi * jnp.exp2(bi - ref)
            ks = jnp.concatenate([k[:lo] * jnp.exp2(ref - b2[:lo]), jnp.zeros((c - lo, dk), F32)], axis=0)
            a = _dot_nt(qs.astype(BF16), ks.astype(BF16))
            top, bot = a[:half], a[half:]
        else:
            top = bot = jnp.zeros((half, c), F32)
        for s in range(SUB):
            ks_row, bs_row = ki[s:s + 1], bi[s:s + 1]
            if s < half:
                p = qi[:half] * ks_row * jnp.exp2(bi[:half] - bs_row)
                top = jnp.where((lane == lo + s) & (row >= s), jnp.sum(p, axis=-1, keepdims=True), top)
            p = qi[half:] * ks_row * jnp.exp2(bi[half:] - bs_row)
            keep = (lane == lo + s) if s < half else (lane == lo + s) & (row >= s - half)
            bot = jnp.where(keep, jnp.sum(p, axis=-1, keepdims=True), bot)
        blocks += [top, bot]
    return jnp.concatenate(blocks, axis=0)


def _chunk_step(q, k, v, b, st):
    c = q.shape[0]
    b2 = b * LOG2_E
    b_end = b2[c - 1:c]
    o = _dot_nt((q * jnp.exp2(b2)).astype(BF16), st.astype(BF16))
    a = _intra_scores(q, k, b2)
    vb = v.astype(BF16)
    o = o + _dot(a.astype(BF16), vb)
    kb = (k * jnp.exp2(b_end - b2)).astype(BF16)
    st_new = st * jnp.exp2(b_end) + _dot_tn(vb, kb)
    return o, st_new


def _gated_rms(o, gate, g):
    o = o * lax.rsqrt(jnp.mean(o * o, axis=-1, keepdims=True) + RMS_EPS) * g
    return o * _silu(gate)


def _gla_kernel(q_ref, k_ref, v_ref, g_ref, la_ref, ng_ref, o_ref, so_ref, st_ref, *, heads, dk, dv):
    j = pl.program_id(1)

    @pl.when(j == 0)
    def _():
        st_ref[...] = jnp.zeros_like(st_ref)

    b_all = _cumsum_rows(la_ref[...])
    scale = dk ** -0.5
    for h in range(heads):
        ks = slice(h * dk, (h + 1) * dk)
        vs = slice(h * dv, (h + 1) * dv)
        o, st = _chunk_step(q_ref[:, ks] * scale, k_ref[:, ks], v_ref[:, vs], b_all[:, ks], st_ref[h])
        st_ref[h] = st
        o_ref[:, vs] = _gated_rms(o, g_ref[:, vs], ng_ref[...]).astype(BF16)

    @pl.when(j == pl.num_programs(1) - 1)
    def _():
        for h in range(heads):
            so_ref[0, h] = st_ref[h].T


def _gla_prompt(z, la, norm_g, nb, t, heads, dk, dv):
    c = min(CHUNK, t)
    n = t // c
    wk, wv = heads * dk, heads * dv
    row = lambda col: (lambda bi, j: (bi * n + j, col))
    kern = functools.partial(_gla_kernel, heads=heads, dk=dk, dv=dv)
    return pl.pallas_call(
        kern, grid=(nb, n),
        in_specs=[pl.BlockSpec((c, wk), row(0)), pl.BlockSpec((c, wk), row(1)),
                  pl.BlockSpec((c, wv), row(1)), pl.BlockSpec((c, wv), row(2)),
                  pl.BlockSpec((c, wk), row(0)), pl.BlockSpec((1, dv), lambda bi, j: (0, 0))],
        out_specs=[pl.BlockSpec((c, wv), row(0)),
                   pl.BlockSpec((1, heads, dk, dv), lambda bi, j: (bi, 0, 0, 0))],
        out_shape=[jax.ShapeDtypeStruct((nb * t, wv), BF16),
                   jax.ShapeDtypeStruct((nb, heads, dk, dv), F32)],
        scratch_shapes=[pltpu.VMEM((heads, dv, dk), F32)],
        compiler_params=_params("parallel", "arbitrary"), name="gla_prompt")(z, z, z, z, la, norm_g)


def _hgrn_inputs(qd, fd, lb):
    f = lb + (1.0 - lb) * _sigmoid(fd)
    return _silu(qd), 1.0 - f, jnp.log(jnp.maximum(f, F_MIN))


def _hgrn_kernel(q_ref, f_ref, v_ref, g_ref, lb_ref, ng_ref, o_ref, so_ref, st_ref, *, heads, dk, dv):
    j = pl.program_id(1)

    @pl.when(j == 0)
    def _():
        st_ref[...] = jnp.zeros_like(st_ref)

    q_all, k_all, lf = _hgrn_inputs(q_ref[...], f_ref[...], lb_ref[...])
    b_all = _cumsum_rows(lf)
    for h in range(heads):
        ks = slice(h * dk, (h + 1) * dk)
        vs = slice(h * dv, (h + 1) * dv)
        o, st = _chunk_step(q_all[:, ks], k_all[:, ks], v_ref[:, vs], b_all[:, ks], st_ref[h])
        st_ref[h] = st
        o_ref[:, vs] = _gated_rms(o, g_ref[:, vs], ng_ref[...]).astype(BF16)

    @pl.when(j == pl.num_programs(1) - 1)
    def _():
        for h in range(heads):
            so_ref[0, h] = st_ref[h].T


def _hgrn_prompt(z, lb, norm_g, nb, t, heads, dk, dv, col0):
    c = min(CHUNK, t)
    n = t // c
    w = heads * dk
    row = lambda col: (lambda bi, j: (bi * n + j, col0 + col))
    kern = functools.partial(_hgrn_kernel, heads=heads, dk=dk, dv=dv)
    return pl.pallas_call(
        kern, grid=(nb, n),
        in_specs=[pl.BlockSpec((c, w), row(0)), pl.BlockSpec((c, w), row(1)),
                  pl.BlockSpec((c, w), row(2)), pl.BlockSpec((c, w), row(3)),
                  pl.BlockSpec((1, w), lambda bi, j: (0, 0)), pl.BlockSpec((1, dv), lambda bi, j: (0, 0))],
        out_specs=[pl.BlockSpec((c, w), lambda bi, j: (bi * n + j, 0)),
                   pl.BlockSpec((1, heads, dk, dv), lambda bi, j: (bi, 0, 0, 0))],
        out_shape=[jax.ShapeDtypeStruct((nb * t, w), BF16),
                   jax.ShapeDtypeStruct((nb, heads, dk, dv), F32)],
        scratch_shapes=[pltpu.VMEM((heads, dv, dk), F32)],
        compiler_params=_params("parallel", "arbitrary"), name="hgrn_prompt")(z, z, z, z, lb, norm_g)


def _columns(x, n):
    w = x.shape[1]
    pad = jnp.concatenate([x, jnp.zeros((w - n, w), F32)], axis=0) if n < w else x
    return pad.T


def _step_heads(q_all, k_all, a_all, v_ref, g_ref, ng_ref, s_ref, o_ref, so_ref, *, heads, dk, dv, tb):
    for h in range(heads):
        ks = slice(h * dk, (h + 1) * dk)
        vs = slice(h * dv, (h + 1) * dv)
        qc, kc, ac = _columns(q_all[:, ks], tb), _columns(k_all[:, ks], tb), _columns(a_all[:, ks], tb)
        v = v_ref[:, vs]
        row = lax.broadcasted_iota(jnp.int32, (tb, dv), 0)
        o = jnp.zeros((tb, dv), F32)
        for n in range(tb):
            s_new = ac[:, n:n + 1] * s_ref[0, n, h] + kc[:, n:n + 1] * v[n:n + 1]
            so_ref[0, n, h] = s_new
            o = jnp.where(row == n, jnp.sum(qc[:, n:n + 1] * s_new, axis=0, keepdims=True), o)
        o_ref[:, vs] = _gated_rms(o, g_ref[:, vs], ng_ref[...]).astype(BF16)


def _gla_step_kernel(q_ref, k_ref, v_ref, g_ref, la_ref, ng_ref, s_ref, prev_ref, o_ref, so_ref,
                     *, heads, dk, dv, tb):
    del prev_ref
    _step_heads(q_ref[...] * dk ** -0.5, k_ref[...], jnp.exp(la_ref[...]), v_ref, g_ref, ng_ref, s_ref,
                o_ref, so_ref, heads=heads, dk=dk, dv=dv, tb=tb)


def _hgrn_step_kernel(q_ref, f_ref, v_ref, g_ref, lb_ref, ng_ref, s_ref, prev_ref, o_ref, so_ref,
                      *, heads, dk, dv, tb):
    del prev_ref
    q_all, k_all, lf = _hgrn_inputs(q_ref[...], f_ref[...], lb_ref[...])
    _step_heads(q_all, k_all, jnp.exp(lf), v_ref, g_ref, ng_ref, s_ref, o_ref, so_ref,
                heads=heads, dk=dk, dv=dv, tb=tb)


def _gla_step(z, la, norm_g, states, new_states, layer):
    m = z.shape[0]
    _, _, heads, dk, dv = states.shape
    tb = 8
    wk, wv = heads * dk, heads * dv
    row = lambda col: (lambda i: (i, col))
    st = pl.BlockSpec((1, tb, heads, dk, dv), lambda i: (layer, i, 0, 0, 0))
    kern = functools.partial(_gla_step_kernel, heads=heads, dk=dk, dv=dv, tb=tb)
    return pl.pallas_call(
        kern, grid=(m // tb,),
        in_specs=[pl.BlockSpec((tb, wk), row(0)), pl.BlockSpec((tb, wk), row(1)),
                  pl.BlockSpec((tb, wv), row(1)), pl.BlockSpec((tb, wv), row(2)),
                  pl.BlockSpec((tb, wk), row(0)), pl.BlockSpec((1, dv), lambda i: (0, 0)), st,
                  pl.BlockSpec(memory_space=pl.ANY)],
        out_specs=[pl.BlockSpec((tb, wv), row(0)), st],
        out_shape=[jax.ShapeDtypeStruct((m, wv), BF16), jax.ShapeDtypeStruct(states.shape, F32)],
        input_output_aliases={7: 1},
        compiler_params=_params("parallel"), name="gla_step")(z, z, z, z, la, norm_g, states, new_states)


def _hgrn_step(z, lb, norm_g, states, new_states, layer, col0):
    m = z.shape[0]
    _, _, heads, dk, dv = states.shape
    tb = 8
    w = heads * dk
    row = lambda col: (lambda i: (i, col0 + col))
    st = pl.BlockSpec((1, tb, heads, dk, dv), lambda i: (layer, i, 0, 0, 0))
    kern = functools.partial(_hgrn_step_kernel, heads=heads, dk=dk, dv=dv, tb=tb)
    return pl.pallas_call(
        kern, grid=(m // tb,),
        in_specs=[pl.BlockSpec((tb, w), row(0)), pl.BlockSpec((tb, w), row(1)),
                  pl.BlockSpec((tb, w), row(2)), pl.BlockSpec((tb, w), row(3)),
                  pl.BlockSpec((1, w), lambda i: (0, 0)), pl.BlockSpec((1, dv), lambda i: (0, 0)), st,
                  pl.BlockSpec(memory_space=pl.ANY)],
        out_specs=[pl.BlockSpec((tb, w), lambda i: (i, 0)), st],
        out_shape=[jax.ShapeDtypeStruct((m, w), BF16), jax.ShapeDtypeStruct(states.shape, F32)],
        input_output_aliases={7: 1},
        compiler_params=_params("parallel"), name="hgrn_step")(z, z, z, z, lb, norm_g, states, new_states)


def _taps(win, w_ref, lanes, first, n_taps):
    n = win.shape[0]
    acc = None
    for res in range(SUBLANES):
        offs = [o for o in range(first, first + n_taps) if o % SUBLANES == res]
        if not offs:
            continue
        sh = win if res == 0 else pltpu.roll(win, n - res, axis=0)
        for o in offs:
            base = o - res
            term = w_ref[o - first:o - first + 1, lanes] * sh[base:base + CONV_ROWS]
            acc = term if acc is None else acc + term
    return acc


def _conv_kernel(gb_ref, gc_ref, hb_ref, ga_ref, gs_ref, sw_ref, cw_ref, cb_ref, lg_ref, lb_ref,
                 ob_ref, oc_ref, so_ref, co_ref, ubuf, cbuf, *, tt, sw, cw):
    t = pl.program_id(1)

    @pl.when(t == 0)
    def _():
        ubuf[0:SCONV_HIST] = jnp.zeros((SCONV_HIST, ubuf.shape[1]), F32)
        cbuf[0:CONF_HIST] = jnp.zeros((CONF_HIST, cbuf.shape[1]), F32)

    ubuf[SCONV_HIST:SCONV_HIST + tt] = gc_ref[...] * hb_ref[...]
    cbuf[CONF_HIST:CONF_HIST + tt] = ga_ref[...] * _sigmoid(gs_ref[...])
    s0 = SCONV_HIST - (sw - 1)
    c0 = CONF_HIST - (cw - 1)

    def strip(r, carry):
        r0 = pl.multiple_of(r * CONV_ROWS, CONV_ROWS)
        yb, yc = [], []
        for lb in range(ubuf.shape[1] // LANES):
            ls = slice(lb * LANES, (lb + 1) * LANES)
            yb.append(_taps(ubuf[pl.ds(r0, CONV_ROWS + SCONV_HIST), ls], sw_ref, ls, s0, sw))
            yc.append(_taps(cbuf[pl.ds(r0, CONV_ROWS + CONF_HIST), ls], cw_ref, ls, c0, cw))
        acc = jnp.concatenate(yb, axis=1)
        ob_ref[pl.ds(r0, CONV_ROWS)] = (gb_ref[pl.ds(r0, CONV_ROWS)] * acc).astype(BF16)
        acc = jnp.concatenate(yc, axis=1)
        y = _layer_norm(acc + cb_ref[...], lg_ref[...], lb_ref[...])
        oc_ref[pl.ds(r0, CONV_ROWS)] = _silu(y).astype(BF16)
        return carry

    lax.fori_loop(0, tt // CONV_ROWS, strip, 0)

    @pl.when(t == pl.num_programs(1) - 1)
    def _():
        so_ref[0] = ubuf[SCONV_HIST + tt - (sw - 1):SCONV_HIST + tt]
        co_ref[0] = cbuf[CONF_HIST + tt - (cw - 1):CONF_HIST + tt]

    ubuf[0:SCONV_HIST] = ubuf[tt:tt + SCONV_HIST]
    cbuf[0:CONF_HIST] = cbuf[tt:tt + CONF_HIST]


def _conv_prompt(z, sconv_w, conf_w, conf_b, ln_g, ln_b, nb, t, d, col0):
    sw, cw = sconv_w.shape[0], conf_w.shape[0]
    tt = _tile(t, 256, CONF_HIST)
    n = t // tt
    row = lambda col: (lambda bi, j: (bi * n + j, col0 + col))
    vec = lambda r: pl.BlockSpec((r, d), lambda bi, j: (0, 0))
    kern = functools.partial(_conv_kernel, tt=tt, sw=sw, cw=cw)
    blk = lambda col: pl.BlockSpec((tt, d), row(col))
    return pl.pallas_call(
        kern, grid=(nb, n),
        in_specs=[blk(0), blk(1), blk(2), blk(3), blk(4), vec(sw), vec(cw), vec(1), vec(1), vec(1)],
        out_specs=[pl.BlockSpec((tt, d), lambda bi, j: (bi * n + j, 0)),
                   pl.BlockSpec((tt, d), lambda bi, j: (bi * n + j, 0)),
                   pl.BlockSpec((1, sw - 1, d), lambda bi, j: (bi, 0, 0)),
                   pl.BlockSpec((1, cw - 1, d), lambda bi, j: (bi, 0, 0))],
        out_shape=[jax.ShapeDtypeStruct((nb * t, d), BF16), jax.ShapeDtypeStruct((nb * t, d), BF16),
                   jax.ShapeDtypeStruct((nb, sw - 1, d), F32), jax.ShapeDtypeStruct((nb, cw - 1, d), F32)],
        scratch_shapes=[pltpu.VMEM((SCONV_HIST + tt, d), F32), pltpu.VMEM((CONF_HIST + tt, d), F32)],
        compiler_params=_params("parallel", "arbitrary"), name="conv_prompt")(
            z, z, z, z, z, sconv_w, conf_w, conf_b, ln_g, ln_b)


def _conv_step_kernel(gb_ref, gc_ref, hb_ref, ga_ref, gs_ref, sc_ref, cc_ref, sw_ref, cw_ref, cb_ref,
                      lg_ref, lb_ref, prev_s_ref, prev_c_ref, ob_ref, oc_ref, ns_ref, nc_ref, *, tb, sw, cw):
    del prev_s_ref, prev_c_ref
    u = gc_ref[...] * hb_ref[...]
    uc = ga_ref[...] * _sigmoid(gs_ref[...])
    row = lax.broadcasted_iota(jnp.int32, u.shape, 0)
    ys = jnp.zeros_like(u)
    yc = jnp.zeros_like(u)
    for n in range(tb):
        ys = jnp.where(row == n, jnp.sum(sc_ref[0, n] * sw_ref[0:sw - 1], axis=0, keepdims=True), ys)
        yc = jnp.where(row == n, jnp.sum(cc_ref[0, n] * cw_ref[0:cw - 1], axis=0, keepdims=True), yc)
        ns_ref[0, n, 0:sw - 2] = sc_ref[0, n, 1:sw - 1]
        ns_ref[0, n, sw - 2:sw - 1] = u[n:n + 1]
        nc_ref[0, n, 0:cw - 2] = cc_ref[0, n, 1:cw - 1]
        nc_ref[0, n, cw - 2:cw - 1] = uc[n:n + 1]
    yb = ys + sw_ref[sw - 1:sw] * u
    ob_ref[...] = (gb_ref[...] * yb).astype(BF16)
    y = yc + cw_ref[cw - 1:cw] * uc + cb_ref[...]
    oc_ref[...] = _silu(_layer_norm(y, lg_ref[...], lb_ref[...])).astype(BF16)


def _conv_step(z, caches_s, caches_c, new_s, new_c, layer, sconv_w, conf_w, conf_b, ln_g, ln_b, d, col0):
    m = z.shape[0]
    sw, cw = sconv_w.shape[0], conf_w.shape[0]
    tb = 16
    blk = lambda col: pl.BlockSpec((tb, d), lambda i: (i, col0 + col))
    vec = lambda r: pl.BlockSpec((r, d), lambda i: (0, 0))
    out = pl.BlockSpec((tb, d), lambda i: (i, 0))
    cs = pl.BlockSpec((1, tb, sw - 1, d), lambda i: (layer, i, 0, 0))
    cc = pl.BlockSpec((1, tb, cw - 1, d), lambda i: (layer, i, 0, 0))
    kern = functools.partial(_conv_step_kernel, tb=tb, sw=sw, cw=cw)
    return pl.pallas_call(
        kern, grid=(m // tb,),
        in_specs=[blk(0), blk(1), blk(2), blk(3), blk(4), cs, cc,
                  vec(sw), vec(cw), vec(1), vec(1), vec(1),
                  pl.BlockSpec(memory_space=pl.ANY), pl.BlockSpec(memory_space=pl.ANY)],
        out_specs=[out, out, cs, cc],
        out_shape=[jax.ShapeDtypeStruct((m, d), BF16), jax.ShapeDtypeStruct((m, d), BF16),
                   jax.ShapeDtypeStruct(caches_s.shape, F32), jax.ShapeDtypeStruct(caches_c.shape, F32)],
        input_output_aliases={12: 2, 13: 3},
        compiler_params=_params("parallel"), name="conv_step")(
            z, z, z, z, z, caches_s, caches_c, sconv_w, conf_w, conf_b, ln_g, ln_b, new_s, new_c)


def _merge_kernel(oa_ref, ob_ref, oc_ref, od_ref, g0_ref, g1_ref, g2_ref, g3_ref, h_ref,
                  wa_ref, wb_ref, wc_ref, wd_ref, wo_ref, lg_ref, lb_ref, prev_ref, o_ref, *, alpha):
    del prev_ref
    merged = _sigmoid(g0_ref[...]) * _dot(oa_ref[...], wa_ref[...])
    merged = merged + _sigmoid(g1_ref[...]) * _dot(ob_ref[...], wb_ref[...])
    merged = merged + _sigmoid(g2_ref[...]) * _dot(oc_ref[...], wc_ref[...])
    merged = merged + _sigmoid(g3_ref[...]) * _dot(od_ref[...], wd_ref[...])
    out = _dot(merged.astype(BF16), wo_ref[...])
    o_ref[...] = _layer_norm(alpha * h_ref[...] + out, lg_ref[...], lb_ref[...])


def _merge(oa, ob, oc, od, z, h, out, wa, wb, wc, wd, wo, ln_g, ln_b, alpha, gate_col0, row0):
    m, d = oa.shape
    tm = _tile(m, 256, 16)
    assert row0 % tm == 0
    r0 = row0 // tm
    row = pl.BlockSpec((tm, d), lambda i: (i, 0))
    mrow = pl.BlockSpec((tm, d), lambda i: (r0 + i, 0))
    gate = lambda c: pl.BlockSpec((tm, d), lambda i: (i, gate_col0 + c))
    wsp = pl.BlockSpec((d, d), lambda i: (0, 0))
    vec = pl.BlockSpec((1, d), lambda i: (0, 0))
    return pl.pallas_call(
        functools.partial(_merge_kernel, alpha=alpha), grid=(m // tm,),
        in_specs=[row, row, row, row, gate(0), gate(1), gate(2), gate(3), mrow, wsp, wsp, wsp, wsp, wsp, vec, vec,
                  pl.BlockSpec(memory_space=pl.ANY)],
        out_specs=mrow, out_shape=jax.ShapeDtypeStruct(out.shape, F32),
        input_output_aliases={16: 0},
        compiler_params=_params("parallel"), name="merge")(
            oa, ob, oc, od, z, z, z, z, h, wa, wb, wc, wd, wo, ln_g, ln_b, out)


def _router_kernel(x_ref, w_ref, b_ref, wt_ref, pos_ref, len_ref, off_ref, base_ref, tot_ref, carry_ref,
                   *, n_exp):
    i = pl.program_id(0)

    @pl.when(i == 0)
    def _():
        carry_ref[...] = jnp.zeros_like(carry_ref)

    tm = x_ref.shape[0]
    logits = _dot(x_ref[...].astype(BF16), w_ref[...].astype(BF16)) + b_ref[...]
    lane = lax.broadcasted_iota(jnp.int32, (tm, LANES), 1)
    lane_f = lane.astype(F32)
    work = jnp.where(lane < n_exp, logits, -jnp.inf)
    vals, hots = [], []
    for kk in range(TOP_K):
        m = jnp.max(work, axis=-1, keepdims=True)
        first = jnp.min(jnp.where(work == m, lane_f, float(LANES)), axis=-1, keepdims=True)
        hot = lane_f == first
        work = jnp.where(hot, -jnp.inf, work)
        vals.append(m)
        hots.append(hot)
    es = [jnp.exp(v - vals[0]) for v in vals]
    den = es[0]
    for e in es[1:]:
        den = den + e
    wt = jnp.zeros((tm, LANES), F32)
    for kk in range(TOP_K):
        wt = jnp.where(lane == kk, es[kk] / den, wt)
    chosen = hots[0]
    for hot in hots[1:]:
        chosen = chosen | hot
    onehot = jnp.where(chosen, 1.0, 0.0)
    r = lax.broadcasted_iota(jnp.int32, (tm, tm), 0)
    c = lax.broadcasted_iota(jnp.int32, (tm, tm), 1)
    before = jnp.where(c < r, 1.0, 0.0).astype(BF16)
    seen = _dot(before, onehot.astype(BF16))
    cnt = jnp.sum(onehot, axis=0, keepdims=True)
    groups = jnp.floor((cnt + (SUBLANES - 1)) * (1.0 / SUBLANES))
    er = lax.broadcasted_iota(jnp.int32, (LANES, LANES), 0)
    ec = lax.broadcasted_iota(jnp.int32, (LANES, LANES), 1)
    earlier = jnp.where(er < ec, 1.0, 0.0).astype(BF16)
    groups8 = jnp.broadcast_to(groups, (SUBLANES, LANES)).astype(BF16)
    toff = _dot(groups8, earlier)[0:1] * float(SUBLANES)
    cnt_pad = groups * float(SUBLANES)
    where_row = toff + seen
    pos = jnp.zeros((tm, LANES), F32)
    for kk in range(TOP_K):
        pk = jnp.sum(jnp.where(hots[kk], where_row, 0.0), axis=-1, keepdims=True)
        pos = jnp.where(lane == kk, pk, pos)
    wt_ref[...] = wt
    pos_ref[...] = pos.astype(jnp.int32)
    len_ref[0] = cnt_pad.astype(jnp.int32)
    off_ref[0] = toff.astype(jnp.int32)
    base_ref[0] = carry_ref[...].astype(jnp.int32)
    carry_ref[...] = carry_ref[...] + cnt_pad
    tot_ref[...] = carry_ref[...].astype(jnp.int32)


def _router(x, w_pad, b_pad, n_exp, tm):
    m, d = x.shape
    row = pl.BlockSpec((tm, LANES), lambda i: (i, 0))
    one = pl.BlockSpec((1, LANES), lambda i: (0, 0))
    meta = pl.BlockSpec((1, 1, LANES), lambda i: (i, 0, 0))
    n = m // tm
    meta_shape = jax.ShapeDtypeStruct((n, 1, LANES), jnp.int32)
    return pl.pallas_call(
        functools.partial(_router_kernel, n_exp=n_exp), grid=(n,),
        in_specs=[pl.BlockSpec((tm, d), lambda i: (i, 0)), pl.BlockSpec((d, LANES), lambda i: (0, 0)), one],
        out_specs=[row, row, meta, meta, meta, one],
        out_shape=[jax.ShapeDtypeStruct((m, LANES), F32), jax.ShapeDtypeStruct((m, LANES), jnp.int32),
                   meta_shape, meta_shape, meta_shape, jax.ShapeDtypeStruct((1, LANES), jnp.int32)],
        scratch_shapes=[pltpu.VMEM((1, LANES), F32)],
        compiler_params=_params("arbitrary"), name="router")(x, w_pad, b_pad)


def _dispatch_kernel(pend_ref, len_ref, off_ref, dst_ref, plen_ref, poff_ref, pdst_ref, x_ref, pos_ref, xs_ref,
                     zbuf, grouped, sem, zsem, *, n_exp, sizes):
    @pl.when(pl.program_id(0) == 0)
    def _():
        zbuf[...] = jnp.zeros_like(zbuf)

        def zero_copy(e):
            first = pl.multiple_of(pend_ref[e] - MOE_ROWS, MOE_ROWS)
            return pltpu.make_async_copy(zbuf, xs_ref.at[pl.ds(first, MOE_ROWS)], zsem)

        def nonempty(e):
            return pend_ref[e] > jnp.where(e > 0, pend_ref[jnp.maximum(e - 1, 0)], 0)

        def zstart(e, carry):
            @pl.when(nonempty(e))
            def _():
                zero_copy(e).start()
            return carry

        def zwait(e, carry):
            @pl.when(nonempty(e))
            def _():
                zero_copy(e).wait()
            return carry

        lax.fori_loop(0, n_exp, zstart, 0)
        lax.fori_loop(0, n_exp, zwait, 0)

        def tail_copy(b):
            return pltpu.make_async_copy(zbuf, xs_ref.at[pl.ds(pl.multiple_of(b * MOE_ROWS, MOE_ROWS), MOE_ROWS)], zsem)

        def tstart(b, carry):
            tail_copy(b).start()
            return carry

        def twait(b, carry):
            tail_copy(b).wait()
            return carry

        first_unused = pend_ref[n_exp - 1] // MOE_ROWS
        lax.fori_loop(first_unused, xs_ref.shape[0] // MOE_ROWS, tstart, 0)
        lax.fori_loop(first_unused, xs_ref.shape[0] // MOE_ROWS, twait, 0)

    i = pl.program_id(0)
    buf = i % 2
    tt = x_ref.shape[0]
    cap = grouped.shape[1]
    slot = lax.broadcasted_iota(jnp.int32, (tt, cap), 1)
    pos = pos_ref[...]
    placed = slot == pos[:, 0:1]
    for kk in range(1, TOP_K):
        placed = placed | (slot == pos[:, kk:kk + 1])
    grouped[buf] = _dot_tn(jnp.where(placed, 1.0, 0.0).astype(BF16), x_ref[...].astype(BF16))

    def piece(meta, b):
        lens, offs, dsts = meta

        def make(e, offset, size):
            src = pl.multiple_of(offs[e] + offset, SUBLANES)
            dst = pl.multiple_of(dsts[e] + offset, SUBLANES)
            return pltpu.make_async_copy(grouped.at[b, pl.ds(src, size)], xs_ref.at[pl.ds(dst, size)], sem.at[b])
        return lens, make

    @pl.when(i > 0)
    def _():
        lens, make = piece((plen_ref, poff_ref, pdst_ref), 1 - buf)
        _for_each_run_piece(lens, n_exp, sizes, lambda *a: make(*a).wait())

    lens, make = piece((len_ref, off_ref, dst_ref), buf)
    _for_each_run_piece(lens, n_exp, sizes, lambda *a: make(*a).start())

    @pl.when(i == pl.num_programs(0) - 1)
    def _():
        _for_each_run_piece(lens, n_exp, sizes, lambda *a: make(*a).wait())


def _run_sizes(tt):
    sizes = [SUBLANES]
    while sizes[-1] * 2 <= tt + SUBLANES - 1:
        sizes.append(sizes[-1] * 2)
    return tuple(reversed(sizes))


def _for_each_run_piece(len_ref, n_exp, sizes, fn):
    def body(e, carry):
        length = len_ref[e]
        for size in sizes:
            @pl.when((length & size) != 0)
            def _():
                fn(e, length & ~(2 * size - 1), size)
        return carry

    lax.fori_loop(0, n_exp, body, 0)


def _grouped_rows(tt, n_exp):
    cap = tt * TOP_K + n_exp * SUBLANES
    return -(-cap // LANES) * LANES


def _dispatch(x, pos, run_len, run_off, run_dst, p_end, rows, tt):
    m, d = x.shape
    n_exp = p_end.shape[0]
    meta = pl.BlockSpec((LANES,), lambda i, pe: (i,), memory_space=pltpu.SMEM)
    prev = pl.BlockSpec((LANES,), lambda i, pe: (jnp.maximum(i - 1, 0),), memory_space=pltpu.SMEM)
    grid_spec = pltpu.PrefetchScalarGridSpec(
        num_scalar_prefetch=1, grid=(m // tt,),
        in_specs=[meta, meta, meta, prev, prev, prev,
                  pl.BlockSpec((tt, d), lambda i, pe: (i, 0)),
                  pl.BlockSpec((tt, LANES), lambda i, pe: (i, 0))],
        out_specs=pl.BlockSpec(memory_space=pl.ANY),
        scratch_shapes=[pltpu.VMEM((MOE_ROWS, d), F32), pltpu.VMEM((2, _grouped_rows(tt, n_exp), d), F32),
                        pltpu.SemaphoreType.DMA((2,)), pltpu.SemaphoreType.DMA(())])
    return pl.pallas_call(
        functools.partial(_dispatch_kernel, n_exp=n_exp, sizes=_run_sizes(tt)), grid_spec=grid_spec,
        out_shape=jax.ShapeDtypeStruct((rows, d), F32),
        compiler_params=_params("arbitrary"), name="moe_dispatch")(
            p_end, run_len, run_off, run_dst, run_len, run_off, run_dst, x, pos)


def _expert_kernel(be_ref, nu_ref, xs_ref, wgu_ref, bgu_ref, wd_ref, bd_ref, y_ref, wgu_bf, wd_bf, *, d_ff):
    i = pl.program_id(0)
    prev = be_ref[jnp.maximum(i - 1, 0)]

    @pl.when((i == 0) | (be_ref[i] != prev))
    def _():
        wgu_bf[...] = wgu_ref[0, 0].astype(BF16)
        wd_bf[...] = wd_ref[0, 0].astype(BF16)

    @pl.when(i < nu_ref[0])
    def _():
        gu = _dot(xs_ref[...].astype(BF16), wgu_bf[...]) + bgu_ref[0]
        x_glu = jnp.minimum(gu[:, :d_ff], SWIGLU_LIMIT)
        x_lin = jnp.clip(gu[:, d_ff:], -SWIGLU_LIMIT, SWIGLU_LIMIT)
        act = x_glu * _sigmoid(SWIGLU_ALPHA * x_glu) * (x_lin + 1.0)
        y_ref[...] = _dot(act.astype(BF16), wd_bf[...]) + bd_ref[0]

    @pl.when(i >= nu_ref[0])
    def _():
        y_ref[...] = jnp.zeros_like(y_ref)


def _experts(xs, block_expert, n_used, w_gate_up, b_gate_up, w_down, b_down, layer):
    rows, d = xs.shape
    _, n_exp, _, ff2 = w_gate_up.shape
    d_ff = ff2 // 2
    n_blocks = rows // MOE_ROWS
    blk = lambda i, be, nu: (jnp.minimum(i, nu[0] - 1), 0)
    grid_spec = pltpu.PrefetchScalarGridSpec(
        num_scalar_prefetch=2, grid=(n_blocks,),
        in_specs=[pl.BlockSpec((MOE_ROWS, d), blk),
                  pl.BlockSpec((1, 1, d, ff2), lambda i, be, nu: (layer, be[i], 0, 0)),
                  pl.BlockSpec((1, 1, ff2), lambda i, be, nu: (be[i], 0, 0)),
                  pl.BlockSpec((1, 1, d_ff, d), lambda i, be, nu: (layer, be[i], 0, 0)),
                  pl.BlockSpec((1, 1, d), lambda i, be, nu: (be[i], 0, 0))],
        out_specs=pl.BlockSpec((MOE_ROWS, d), lambda i, be, nu: (i, 0)),
        scratch_shapes=[pltpu.VMEM((d, ff2), BF16), pltpu.VMEM((d_ff, d), BF16)])
    return pl.pallas_call(
        functools.partial(_expert_kernel, d_ff=d_ff), grid_spec=grid_spec,
        out_shape=jax.ShapeDtypeStruct((rows, d), F32),
        compiler_params=_params("arbitrary"), name="moe_experts")(
            block_expert, n_used, xs, w_gate_up, b_gate_up.reshape(n_exp, 1, ff2),
            w_down, b_down.reshape(n_exp, 1, d))


def _combine_kernel(len_ref, off_ref, dst_ref, nlen_ref, noff_ref, ndst_ref, pos_ref, wt_ref, h_ref, lg_ref, lb_ref,
                    y_ref, o_ref, ob_ref, rows, moe, sem, *, n_exp, sizes, alpha):
    i = pl.program_id(0)
    buf = i % 2

    def piece(meta, b):
        lens, offs, dsts = meta

        def make(e, offset, size):
            src = pl.multiple_of(dsts[e] + offset, SUBLANES)
            dst = pl.multiple_of(offs[e] + offset, SUBLANES)
            return pltpu.make_async_copy(y_ref.at[pl.ds(src, size)], rows.at[b, pl.ds(dst, size)], sem.at[b])
        return lens, make

    lens, make = piece((len_ref, off_ref, dst_ref), buf)

    @pl.when(i == 0)
    def _():
        _for_each_run_piece(lens, n_exp, sizes, lambda *a: make(*a).start())

    @pl.when(i + 1 < pl.num_programs(0))
    def _():
        nlens, nmake = piece((nlen_ref, noff_ref, ndst_ref), 1 - buf)
        _for_each_run_piece(nlens, n_exp, sizes, lambda *a: nmake(*a).start())

    _for_each_run_piece(lens, n_exp, sizes, lambda *a: make(*a).wait())

    def token(n, carry):
        acc = wt_ref[n * TOP_K] * rows[buf, pl.ds(pos_ref[n * TOP_K], 1), :]
        for kk in range(1, TOP_K):
            acc = acc + wt_ref[n * TOP_K + kk] * rows[buf, pl.ds(pos_ref[n * TOP_K + kk], 1), :]
        moe[pl.ds(n, 1), :] = acc
        return carry

    lax.fori_loop(0, h_ref.shape[0], token, 0, unroll=2)
    y = _layer_norm(alpha * h_ref[...] + moe[...], lg_ref[...], lb_ref[...])
    o_ref[...] = y
    ob_ref[...] = y.astype(BF16)


def _combine(y, pos_flat, wt_flat, run_len, run_off, run_dst, h, ln_g, ln_b, alpha, n_exp, tt):
    m, d = h.shape
    vec = pl.BlockSpec((1, d), lambda i: (0, 0))
    row = pl.BlockSpec((tt, d), lambda i: (i, 0))
    n_tiles = m // tt
    meta = pl.BlockSpec((LANES,), lambda i: (i,), memory_space=pltpu.SMEM)
    nxt = pl.BlockSpec((LANES,), lambda i: (jnp.minimum(i + 1, n_tiles - 1),), memory_space=pltpu.SMEM)
    per_tok = pl.BlockSpec((pos_flat.shape[0] // n_tiles,), lambda i: (i,), memory_space=pltpu.SMEM)
    return pl.pallas_call(
        functools.partial(_combine_kernel, n_exp=n_exp, sizes=_run_sizes(tt), alpha=alpha), grid=(n_tiles,),
        in_specs=[meta, meta, meta, nxt, nxt, nxt, per_tok, per_tok, row, vec, vec,
                  pl.BlockSpec(memory_space=pl.ANY)],
        out_specs=[row, row],
        out_shape=[jax.ShapeDtypeStruct((m, d), F32), jax.ShapeDtypeStruct((m, d), BF16)],
        scratch_shapes=[pltpu.VMEM((2, _grouped_rows(tt, n_exp), d), F32), pltpu.VMEM((tt, d), F32),
                        pltpu.SemaphoreType.DMA((2,))],
        compiler_params=_params("arbitrary"), name="moe_combine")(
            run_len, run_off, run_dst, run_len, run_off, run_dst, pos_flat, wt_flat, h, ln_g, ln_b, y)


def _moe(h, w, alpha, layer):
    m, d = h.shape
    n_exp = w["n_exp"]
    tt = _tile(m, 384, 16)
    n_tiles = m // tt
    wt, pos, run_len, run_off, run_base, totals = _router(h, w["w_router_pad"], w["b_router_pad"], n_exp, tt)
    seg = totals[0, :n_exp]
    padded = (seg + MOE_ROWS - 1) // MOE_ROWS * MOE_ROWS
    p_end = jnp.cumsum(padded).astype(jnp.int32)
    p_start = jnp.zeros((LANES,), jnp.int32).at[:n_exp].set(p_end - padded)
    run_dst = (run_base[:, 0, :] + p_start[None, :]).reshape(-1)
    run_len, run_off = run_len.reshape(-1), run_off.reshape(-1)
    n_blocks = -(-(m * TOP_K + n_tiles * n_exp * (SUBLANES - 1)) // MOE_ROWS) + n_exp
    block_start = jnp.arange(n_blocks, dtype=jnp.int32) * MOE_ROWS
    block_expert = jnp.minimum(jnp.sum(p_end[None, :] <= block_start[:, None], axis=1), n_exp - 1).astype(jnp.int32)
    n_used = p_end[-1:] // MOE_ROWS
    xs = _dispatch(h, pos, run_len, run_off, run_dst, p_end, n_blocks * MOE_ROWS, tt)
    y = _experts(xs, block_expert, n_used, w["w_gate_up"], w["b_gate_up"], w["w_down"], w["b_down"], layer)
    chunk = max(LANES, 1 << (tt * TOP_K - 1).bit_length())
    per_tile = lambda a: jnp.pad(a[:, :TOP_K].reshape(n_tiles, tt * TOP_K),
                                 ((0, 0), (0, chunk - tt * TOP_K))).reshape(-1)
    return _combine(y, per_tile(pos), per_tile(wt), run_len, run_off, run_dst,
                    h, w["ln2_g"], w["ln2_b"], alpha, n_exp, tt)


def _mixers(h, hb, h1, w, dims, alpha, states, layer, row0):
    nb, t, d, gh, gdk, gdv, hh, hdk, hdv = dims
    z = _matmul(hb, w["w_in_main"], layer, row0, nb * t)
    la = _gla_decay(hb, w["w_lra"], w["w_gla_lr"], w["b_gla_lr"], row0, nb * t)
    col_conv = (2 * gh * gdk + 2 * gh * gdv) // d
    col_hgrn = col_conv + 5
    col_gate = col_hgrn + 4
    if states is None:
        oa, s_gla = _gla_prompt(z, la, w["gla_norm_g"], nb, t, gh, gdk, gdv)
        od, s_hgrn = _hgrn_prompt(z, w["hgrn_lb"], w["hgrn_norm_g"], nb, t, hh, hdk, hdv, col_hgrn)
        ob, oc, c_s, c_c = _conv_prompt(z, w["sconv_w"], w["conf_conv_w"], w["conf_conv_b"],
                                        w["conf_ln_g"], w["conf_ln_b"], nb, t, d, col_conv)
    else:
        st_gla, st_hgrn, new_gla, new_hgrn, caches_s, caches_c, new_s, new_c = states
        oa, s_gla = _gla_step(z, la, w["gla_norm_g"], st_gla, new_gla, layer)
        od, s_hgrn = _hgrn_step(z, w["hgrn_lb"], w["hgrn_norm_g"], st_hgrn, new_hgrn, layer, col_hgrn)
        ob, oc, c_s, c_c = _conv_step(z, caches_s, caches_c, new_s, new_c, layer, w["sconv_w"], w["conf_conv_w"],
                                      w["conf_conv_b"], w["conf_ln_g"], w["conf_ln_b"], d, col_conv)
    h1 = _merge(oa, ob, oc, od, z, h, h1, w["w_br_a"], w["w_br_b"], w["w_br_c"], w["w_br_d"], w["w_o"],
                w["ln1_g"], w["ln1_b"], alpha, col_gate, row0)
    return h1, (s_gla, s_hgrn, c_s, c_c)


def kernel(x_prompt, x_sample, state_gla, state_hgrn, cache_sconv, cache_conformer, ln_in_g, ln_in_b, w_in, w_gla_lr, b_gla_lr, gla_norm_g, w_br_a, sconv_w, w_br_b, conf_conv_w, conf_conv_b, conf_ln_g, conf_ln_b, w_br_c, hgrn_lb_logits, hgrn_norm_g, w_br_d, w_o, ln1_g, ln1_b, w_router, b_router, w_gate_up, b_gate_up, w_down, b_down, ln2_g, ln2_b):
    depth, d, _ = w_in.shape
    _, _, gh, gdk, gdv = state_gla.shape
    _, _, hh, hdk, hdv = state_hgrn.shape
    rank = w_gla_lr.shape[1]
    n_exp = w_router.shape[2]
    alpha = (2 * depth) ** 0.25
    lra0 = 2 * gh * gdk + 2 * gh * gdv

    lb_p = jax.nn.softmax(hgrn_lb_logits.astype(F32), axis=0)
    hgrn_lb = jnp.cumsum(lb_p, axis=0) - lb_p[:1]
    row = lambda a: a.reshape(1, -1)
    w_in_main = jnp.concatenate([w_in[:, :, :lra0], w_in[:, :, lra0 + rank:]], axis=2).astype(BF16)
    layers = []
    for l in range(depth):
        w_lra = jnp.zeros((d, LANES), F32).at[:, :rank].set(w_in[l, :, lra0:lra0 + rank])
        w2 = jnp.zeros((LANES, gh * gdk), F32).at[:rank].set(w_gla_lr[l])
        layers.append({
            "w_in_main": w_in_main,
            "w_lra": w_lra.astype(BF16), "w_gla_lr": w2.astype(BF16), "b_gla_lr": row(b_gla_lr[l]),
            "gla_norm_g": row(gla_norm_g[l]), "hgrn_norm_g": row(hgrn_norm_g[l]), "hgrn_lb": row(hgrn_lb[l]),
            "sconv_w": sconv_w[l], "conf_conv_w": conf_conv_w[l], "conf_conv_b": row(conf_conv_b[l]),
            "conf_ln_g": row(conf_ln_g[l]), "conf_ln_b": row(conf_ln_b[l]),
            "w_br_a": w_br_a[l].astype(BF16), "w_br_b": w_br_b[l].astype(BF16),
            "w_br_c": w_br_c[l].astype(BF16), "w_br_d": w_br_d[l].astype(BF16), "w_o": w_o[l].astype(BF16),
            "ln1_g": row(ln1_g[l]), "ln1_b": row(ln1_b[l]), "ln2_g": row(ln2_g[l]), "ln2_b": row(ln2_b[l]),
            "n_exp": n_exp,
            "w_router_pad": jnp.zeros((d, LANES), F32).at[:, :n_exp].set(w_router[l]),
            "b_router_pad": jnp.zeros((1, LANES), F32).at[0, :n_exp].set(b_router[l]),
            "w_gate_up": w_gate_up, "b_gate_up": b_gate_up[l], "w_down": w_down, "b_down": b_down[l],
        })

    nbp, tp, _ = x_prompt.shape
    nbs, ts, _ = x_sample.shape
    assert ts == 1, "the sample group advances its states by exactly one token"
    dims_p = (nbp, tp, d, gh, gdk, gdv, hh, hdk, hdv)
    dims_s = (nbs, ts, d, gh, gdk, gdv, hh, hdk, hdv)
    mp, ms = nbp * tp, nbs * ts
    h, hb = _ln_call(jnp.concatenate([x_prompt.reshape(mp, d), x_sample.reshape(ms, d)], axis=0), ln_in_g, ln_in_b)
    prompt_states = ([], [], [], [])
    new_gla, new_hgrn = jnp.zeros_like(state_gla), jnp.zeros_like(state_hgrn)
    new_s, new_c = jnp.zeros_like(cache_sconv), jnp.zeros_like(cache_conformer)
    for l, w in enumerate(layers):
        h1 = jnp.zeros_like(h)
        h1, st_p = _mixers(h, hb, h1, w, dims_p, alpha, None, l, 0)
        h1, (new_gla, new_hgrn, new_s, new_c) = _mixers(
            h, hb, h1, w, dims_s, alpha,
            (state_gla, state_hgrn, new_gla, new_hgrn, cache_sconv, cache_conformer, new_s, new_c), l, mp)
        h, hb = _moe(h1, w, alpha, l)
        for acc, s in zip(prompt_states, st_p):
            acc.append(s)
    y_p, y_s = h[:mp].reshape(nbp, tp, d), h[mp:].reshape(nbs, ts, d)
    return ((y_p, y_s) + tuple(jnp.stack(a) for a in prompt_states) + (new_gla, new_hgrn, new_s, new_c))
```

```python
import functools

import jax
import jax.numpy as jnp
from jax import lax
from jax.experimental import pallas as pl
from jax.experimental.pallas import tpu as pltpu

F32 = jnp.float32
BF16 = jnp.bfloat16

TOP_K = 4
CHUNK = 64
SUB = 16
GLA_GATE_NORM = 16.0
F_MIN = 1e-20
LN_EPS = 1e-5
RMS_EPS = 1e-6
SWIGLU_ALPHA = 1.702
SWIGLU_LIMIT = 7.0
LOG2_E = 1.4426950408889634
MOE_ROWS = 512
LANES = 128
SUBLANES = 8
CONV_ROWS = 32
CONF_HIST = 32
SCONV_HIST = 8
V7X_VMEM_LIMIT = 56 * 1024 * 1024


def _params(*sem):
    return pltpu.CompilerParams(dimension_semantics=sem, vmem_limit_bytes=V7X_VMEM_LIMIT)


def _tile(n, pref, mult=8):
    if n <= pref:
        return n
    t = pref - pref % mult
    while t >= mult:
        if n % t == 0:
            return t
        t -= mult
    raise ValueError(f"no tile for {n}")


def _sigmoid(x):
    return 1.0 / (1.0 + jnp.exp(-x))


def _silu(x):
    return x * _sigmoid(x)


def _log_sigmoid(x):
    return jnp.minimum(x, 0.0) - jnp.log(1.0 + jnp.exp(-jnp.abs(x)))


def _layer_norm(x, g, b):
    mu = jnp.mean(x, axis=-1, keepdims=True)
    xc = x - mu
    var = jnp.mean(xc * xc, axis=-1, keepdims=True)
    return xc * lax.rsqrt(var + LN_EPS) * g + b


def _dot(a, b):
    return jnp.dot(a, b, preferred_element_type=F32)


def _dot_nt(a, b):
    return lax.dot_general(a, b, (((1,), (1,)), ((), ())), preferred_element_type=F32)


def _dot_tn(a, b):
    return lax.dot_general(a, b, (((0,), (0,)), ((), ())), preferred_element_type=F32)


def _split3(x):
    hi = x.astype(BF16)
    r = x - hi.astype(F32)
    mid = r.astype(BF16)
    lo = (r - mid.astype(F32)).astype(BF16)
    return hi, mid, lo


def _ln_kernel(x_ref, g_ref, b_ref, o_ref, ob_ref):
    y = _layer_norm(x_ref[...], g_ref[...], b_ref[...])
    o_ref[...] = y
    ob_ref[...] = y.astype(BF16)


def _ln_call(x, g, b):
    m, d = x.shape
    tm = _tile(m, 512, 16)
    row = pl.BlockSpec((tm, d), lambda i: (i, 0))
    vec = pl.BlockSpec((1, d), lambda i: (0, 0))
    return pl.pallas_call(
        _ln_kernel, grid=(m // tm,), in_specs=[row, vec, vec], out_specs=[row, row],
        out_shape=[jax.ShapeDtypeStruct((m, d), F32), jax.ShapeDtypeStruct((m, d), BF16)],
        compiler_params=_params("parallel"), name="ln_in")(x, g.reshape(1, d), b.reshape(1, d))


def _mm_kernel(x_ref, w_ref, o_ref):
    o_ref[...] = _dot(x_ref[...], w_ref[0])


def _matmul(x, w, layer, row0, m):
    k = x.shape[1]
    n = w.shape[2]
    tm = _tile(m, 1024, 16)
    tn = _tile(n, 2048, LANES)
    assert row0 % tm == 0
    r0 = row0 // tm
    return pl.pallas_call(
        _mm_kernel, grid=(n // tn, m // tm),
        in_specs=[pl.BlockSpec((tm, k), lambda j, i: (r0 + i, 0)),
                  pl.BlockSpec((1, k, tn), lambda j, i: (layer, 0, j))],
        out_specs=pl.BlockSpec((tm, tn), lambda j, i: (i, j)),
        out_shape=jax.ShapeDtypeStruct((m, n), F32),
        compiler_params=_params("parallel", "parallel"), name="in_proj")(x, w)


def _gla_decay_kernel(h_ref, w1_ref, w2_ref, b2_ref, o_ref):
    lra = _dot(h_ref[...], w1_ref[...])
    x = _dot(lra.astype(BF16), w2_ref[...]) + b2_ref[...]
    o_ref[...] = _log_sigmoid(x) * (1.0 / GLA_GATE_NORM)


def _gla_decay(hb, w1, w2, b2, row0, m):
    d = hb.shape[1]
    n = w2.shape[1]
    tm = _tile(m, 512, 16)
    assert row0 % tm == 0
    r0 = row0 // tm
    return pl.pallas_call(
        _gla_decay_kernel, grid=(m // tm,),
        in_specs=[pl.BlockSpec((tm, d), lambda i: (r0 + i, 0)), pl.BlockSpec(w1.shape, lambda i: (0, 0)),
                  pl.BlockSpec(w2.shape, lambda i: (0, 0)), pl.BlockSpec((1, n), lambda i: (0, 0))],
        out_specs=pl.BlockSpec((tm, n), lambda i: (i, 0)),
        out_shape=jax.ShapeDtypeStruct((m, n), F32),
        compiler_params=_params("parallel"), name="gla_decay")(hb, w1, w2, b2)


def _cumsum_rows(x):
    c = x.shape[0]
    r = lax.broadcasted_iota(jnp.int32, (c, c), 0)
    s = lax.broadcasted_iota(jnp.int32, (c, c), 1)
    tri = jnp.where(r >= s, 1.0, 0.0).astype(BF16)
    hi, mid, lo = _split3(x)
    return _dot(tri, hi) + _dot(tri, mid) + _dot(tri, lo)


def _intra_scores(q, k, b2):
    c, dk = q.shape
    half = SUB // 2
    lane = lax.broadcasted_iota(jnp.int32, (half, c), 1)
    row = lax.broadcasted_iota(jnp.int32, (half, c), 0)
    blocks = []
    for i in range(c // SUB):
        lo = i * SUB
        qi, ki, bi = q[lo:lo + SUB], k[lo:lo + SUB], b2[lo:lo + SUB]
        if i > 0:
            ref = b2[lo - 1:lo]
            qs = qi * jnp.exp2(bi - ref)
            ks = jnp.concatenate([k[:lo] * jnp.exp2(ref - b2[:lo]), jnp.zeros((c - lo, dk), F32)], axis=0)
            a = _dot_nt(qs.astype(BF16), ks.astype(BF16))
            top, bot = a[:half], a[half:]
        else:
            top = bot = jnp.zeros((half, c), F32)
        for s in range(SUB):
            ks_row, bs_row = ki[s:s + 1], bi[s:s + 1]
            if s < half:
                p = qi[:half] * ks_row * jnp.exp2(bi[:half] - bs_row)
                top = jnp.where((lane == lo + s) & (row >= s), jnp.sum(p, axis=-1, keepdims=True), top)
            p = qi[half:] * ks_row * jnp.exp2(bi[half:] - bs_row)
            keep = (lane == lo + s) if s < half else (lane == lo + s) & (row >= s - half)
            bot = jnp.where(keep, jnp.sum(p, axis=-1, keepdims=True), bot)
        blocks += [top, bot]
    return jnp.concatenate(blocks, axis=0)


def _chunk_step(q, k, v, b, st):
    c = q.shape[0]
    b2 = b * LOG2_E
    b_end = b2[c - 1:c]
    o = _dot_nt((q * jnp.exp2(b2)).astype(BF16), st.astype(BF16))
    a = _intra_scores(q, k, b2)
    vb = v.astype(BF16)
    o = o + _dot(a.astype(BF16), vb)
    kb = (k * jnp.exp2(b_end - b2)).astype(BF16)
    st_new = st * jnp.exp2(b_end) + _dot_tn(vb, kb)
    return o, st_new


def _gated_rms(o, gate, g):
    o = o * lax.rsqrt(jnp.mean(o * o, axis=-1, keepdims=True) + RMS_EPS) * g
    return o * _silu(gate)


def _gla_kernel(q_ref, k_ref, v_ref, g_ref, la_ref, ng_ref, o_ref, so_ref, st_ref, *, heads, dk, dv):
    j = pl.program_id(1)

    @pl.when(j == 0)
    def _():
        st_ref[...] = jnp.zeros_like(st_ref)

    scale = dk ** -0.5
    for s in range(q_ref.shape[0]):
        b_all = _cumsum_rows(la_ref[s])
        for h in range(heads):
            ks = slice(h * dk, (h + 1) * dk)
            vs = slice(h * dv, (h + 1) * dv)
            o, st = _chunk_step(q_ref[s, :, ks] * scale, k_ref[s, :, ks], v_ref[s, :, vs], b_all[:, ks],
                                st_ref[s, h])
            st_ref[s, h] = st
            o_ref[s, :, vs] = _gated_rms(o, g_ref[s, :, vs], ng_ref[...]).astype(BF16)

    @pl.when(j == pl.num_programs(1) - 1)
    def _():
        for s in range(q_ref.shape[0]):
            for h in range(heads):
                so_ref[s, h] = st_ref[s, h].T


def _seq_group(nb):
    return next(g for g in (4, 2, 1) if nb % g == 0)


def _gla_prompt(z, la, norm_g, nb, t, heads, dk, dv):
    c = min(CHUNK, t)
    n = t // c
    grp = _seq_group(nb)
    wk, wv = heads * dk, heads * dv
    z3, la3 = z.reshape(nb, t, z.shape[1]), la.reshape(nb, t, la.shape[1])
    row = lambda col: (lambda bi, j: (bi, j, col))
    kern = functools.partial(_gla_kernel, heads=heads, dk=dk, dv=dv)
    o, st = pl.pallas_call(
        kern, grid=(nb // grp, n),
        in_specs=[pl.BlockSpec((grp, c, wk), row(0)), pl.BlockSpec((grp, c, wk), row(1)),
                  pl.BlockSpec((grp, c, wv), row(1)), pl.BlockSpec((grp, c, wv), row(2)),
                  pl.BlockSpec((grp, c, wk), row(0)), pl.BlockSpec((1, dv), lambda bi, j: (0, 0))],
        out_specs=[pl.BlockSpec((grp, c, wv), row(0)),
                   pl.BlockSpec((grp, heads, dk, dv), lambda bi, j: (bi, 0, 0, 0))],
        out_shape=[jax.ShapeDtypeStruct((nb, t, wv), BF16),
                   jax.ShapeDtypeStruct((nb, heads, dk, dv), F32)],
        scratch_shapes=[pltpu.VMEM((grp, heads, dv, dk), F32)],
        compiler_params=_params("parallel", "arbitrary"), name="gla_prompt")(z3, z3, z3, z3, la3, norm_g)
    return o.reshape(nb * t, wv), st


def _hgrn_inputs(qd, fd, lb):
    f = lb + (1.0 - lb) * _sigmoid(fd)
    return _silu(qd), 1.0 - f, jnp.log(jnp.maximum(f, F_MIN))


def _hgrn_kernel(q_ref, f_ref, v_ref, g_ref, lb_ref, ng_ref, o_ref, so_ref, st_ref, *, heads, dk, dv):
    j = pl.program_id(1)

    @pl.when(j == 0)
    def _():
        st_ref[...] = jnp.zeros_like(st_ref)

    for s in range(q_ref.shape[0]):
        q_all, k_all, lf = _hgrn_inputs(q_ref[s], f_ref[s], lb_ref[...])
        b_all = _cumsum_rows(lf)
        for h in range(heads):
            ks = slice(h * dk, (h + 1) * dk)
            vs = slice(h * dv, (h + 1) * dv)
            o, st = _chunk_step(q_all[:, ks], k_all[:, ks], v_ref[s, :, vs], b_all[:, ks], st_ref[s, h])
            st_ref[s, h] = st
            o_ref[s, :, vs] = _gated_rms(o, g_ref[s, :, vs], ng_ref[...]).astype(BF16)

    @pl.when(j == pl.num_programs(1) - 1)
    def _():
        for s in range(q_ref.shape[0]):
            for h in range(heads):
                so_ref[s, h] = st_ref[s, h].T


def _hgrn_prompt(z, lb, norm_g, nb, t, heads, dk, dv, col0):
    c = min(CHUNK, t)
    n = t // c
    grp = _seq_group(nb)
    w = heads * dk
    z3 = z.reshape(nb, t, z.shape[1])
    row = lambda col: (lambda bi, j: (bi, j, col0 + col))
    kern = functools.partial(_hgrn_kernel, heads=heads, dk=dk, dv=dv)
    o, st = pl.pallas_call(
        kern, grid=(nb // grp, n),
        in_specs=[pl.BlockSpec((grp, c, w), row(0)), pl.BlockSpec((grp, c, w), row(1)),
                  pl.BlockSpec((grp, c, w), row(2)), pl.BlockSpec((grp, c, w), row(3)),
                  pl.BlockSpec((1, w), lambda bi, j: (0, 0)), pl.BlockSpec((1, dv), lambda bi, j: (0, 0))],
        out_specs=[pl.BlockSpec((grp, c, w), lambda bi, j: (bi, j, 0)),
                   pl.BlockSpec((grp, heads, dk, dv), lambda bi, j: (bi, 0, 0, 0))],
        out_shape=[jax.ShapeDtypeStruct((nb, t, w), BF16),
                   jax.ShapeDtypeStruct((nb, heads, dk, dv), F32)],
        scratch_shapes=[pltpu.VMEM((grp, heads, dv, dk), F32)],
        compiler_params=_params("parallel", "arbitrary"), name="hgrn_prompt")(z3, z3, z3, z3, lb, norm_g)
    return o.reshape(nb * t, w), st


def _columns(x, n):
    w = x.shape[1]
    pad = jnp.concatenate([x, jnp.zeros((w - n, w), F32)], axis=0) if n < w else x
    return pad.T


def _step_heads(q_all, k_all, a_all, v_ref, g_ref, ng_ref, s_ref, o_ref, so_ref, *, heads, dk, dv, tb):
    for h in range(heads):
        ks = slice(h * dk, (h + 1) * dk)
        vs = slice(h * dv, (h + 1) * dv)
        qc, kc, ac = _columns(q_all[:, ks], tb), _columns(k_all[:, ks], tb), _columns(a_all[:, ks], tb)
        v = v_ref[:, vs]
        row = lax.broadcasted_iota(jnp.int32, (tb, dv), 0)
        o = jnp.zeros((tb, dv), F32)
        for n in range(tb):
            s_new = ac[:, n:n + 1] * s_ref[0, n, h] + kc[:, n:n + 1] * v[n:n + 1]
            so_ref[0, n, h] = s_new
            o = jnp.where(row == n, jnp.sum(qc[:, n:n + 1] * s_new, axis=0, keepdims=True), o)
        o_ref[:, vs] = _gated_rms(o, g_ref[:, vs], ng_ref[...]).astype(BF16)


def _gla_step_kernel(q_ref, k_ref, v_ref, g_ref, la_ref, ng_ref, s_ref, prev_ref, o_ref, so_ref,
                     *, heads, dk, dv, tb):
    del prev_ref
    _step_heads(q_ref[...] * dk ** -0.5, k_ref[...], jnp.exp(la_ref[...]), v_ref, g_ref, ng_ref, s_ref,
                o_ref, so_ref, heads=heads, dk=dk, dv=dv, tb=tb)


def _hgrn_step_kernel(q_ref, f_ref, v_ref, g_ref, lb_ref, ng_ref, s_ref, prev_ref, o_ref, so_ref,
                      *, heads, dk, dv, tb):
    del prev_ref
    q_all, k_all, lf = _hgrn_inputs(q_ref[...], f_ref[...], lb_ref[...])
    _step_heads(q_all, k_all, jnp.exp(lf), v_ref, g_ref, ng_ref, s_ref, o_ref, so_ref,
                heads=heads, dk=dk, dv=dv, tb=tb)


def _gla_step(z, la, norm_g, states, new_states, layer):
    m = z.shape[0]
    _, _, heads, dk, dv = states.shape
    tb = 8
    wk, wv = heads * dk, heads * dv
    row = lambda col: (lambda i: (i, col))
    st = pl.BlockSpec((1, tb, heads, dk, dv), lambda i: (layer, i, 0, 0, 0))
    kern = functools.partial(_gla_step_kernel, heads=heads, dk=dk, dv=dv, tb=tb)
    return pl.pallas_call(
        kern, grid=(m // tb,),
        in_specs=[pl.BlockSpec((tb, wk), row(0)), pl.BlockSpec((tb, wk), row(1)),
                  pl.BlockSpec((tb, wv), row(1)), pl.BlockSpec((tb, wv), row(2)),
                  pl.BlockSpec((tb, wk), row(0)), pl.BlockSpec((1, dv), lambda i: (0, 0)), st,
                  pl.BlockSpec(memory_space=pl.ANY)],
        out_specs=[pl.BlockSpec((tb, wv), row(0)), st],
        out_shape=[jax.ShapeDtypeStruct((m, wv), BF16), jax.ShapeDtypeStruct(states.shape, F32)],
        input_output_aliases={7: 1},
        compiler_params=_params("parallel"), name="gla_step")(z, z, z, z, la, norm_g, states, new_states)


def _hgrn_step(z, lb, norm_g, states, new_states, layer, col0):
    m = z.shape[0]
    _, _, heads, dk, dv = states.shape
    tb = 8
    w = heads * dk
    row = lambda col: (lambda i: (i, col0 + col))
    st = pl.BlockSpec((1, tb, heads, dk, dv), lambda i: (layer, i, 0, 0, 0))
    kern = functools.partial(_hgrn_step_kernel, heads=heads, dk=dk, dv=dv, tb=tb)
    return pl.pallas_call(
        kern, grid=(m // tb,),
        in_specs=[pl.BlockSpec((tb, w), row(0)), pl.BlockSpec((tb, w), row(1)),
                  pl.BlockSpec((tb, w), row(2)), pl.BlockSpec((tb, w), row(3)),
                  pl.BlockSpec((1, w), lambda i: (0, 0)), pl.BlockSpec((1, dv), lambda i: (0, 0)), st,
                  pl.BlockSpec(memory_space=pl.ANY)],
        out_specs=[pl.BlockSpec((tb, w), lambda i: (i, 0)), st],
        out_shape=[jax.ShapeDtypeStruct((m, w), BF16), jax.ShapeDtypeStruct(states.shape, F32)],
        input_output_aliases={7: 1},
        compiler_params=_params("parallel"), name="hgrn_step")(z, z, z, z, lb, norm_g, states, new_states)


def _taps(win, w_ref, lanes, first, n_taps):
    n = win.shape[0]
    acc = None
    for res in range(SUBLANES):
        offs = [o for o in range(first, first + n_taps) if o % SUBLANES == res]
        if not offs:
            continue
        sh = win if res == 0 else pltpu.roll(win, n - res, axis=0)
        for o in offs:
            base = o - res
            term = w_ref[o - first:o - first + 1, lanes] * sh[base:base + CONV_ROWS]
            acc = term if acc is None else acc + term
    return acc


def _conv_kernel(gb_ref, gc_ref, hb_ref, ga_ref, gs_ref, sw_ref, cw_ref, cb_ref, lg_ref, lb_ref,
                 ob_ref, oc_ref, so_ref, co_ref, ubuf, cbuf, *, tt, sw, cw):
    t = pl.program_id(1)

    @pl.when(t == 0)
    def _():
        ubuf[0:SCONV_HIST] = jnp.zeros((SCONV_HIST, ubuf.shape[1]), F32)
        cbuf[0:CONF_HIST] = jnp.zeros((CONF_HIST, cbuf.shape[1]), F32)

    ubuf[SCONV_HIST:SCONV_HIST + tt] = gc_ref[...] * hb_ref[...]
    cbuf[CONF_HIST:CONF_HIST + tt] = ga_ref[...] * _sigmoid(gs_ref[...])
    s0 = SCONV_HIST - (sw - 1)
    c0 = CONF_HIST - (cw - 1)

    def strip(r, carry):
        r0 = pl.multiple_of(r * CONV_ROWS, CONV_ROWS)
        yb, yc = [], []
        for lb in range(ubuf.shape[1] // LANES):
            ls = slice(lb * LANES, (lb + 1) * LANES)
            yb.append(_taps(ubuf[pl.ds(r0, CONV_ROWS + SCONV_HIST), ls], sw_ref, ls, s0, sw))
            yc.append(_taps(cbuf[pl.ds(r0, CONV_ROWS + CONF_HIST), ls], cw_ref, ls, c0, cw))
        acc = jnp.concatenate(yb, axis=1)
        ob_ref[pl.ds(r0, CONV_ROWS)] = (gb_ref[pl.ds(r0, CONV_ROWS)] * acc).astype(BF16)
        acc = jnp.concatenate(yc, axis=1)
        y = _layer_norm(acc + cb_ref[...], lg_ref[...], lb_ref[...])
        oc_ref[pl.ds(r0, CONV_ROWS)] = _silu(y).astype(BF16)
        return carry

    lax.fori_loop(0, tt // CONV_ROWS, strip, 0)

    @pl.when(t == pl.num_programs(1) - 1)
    def _():
        so_ref[0] = ubuf[SCONV_HIST + tt - (sw - 1):SCONV_HIST + tt]
        co_ref[0] = cbuf[CONF_HIST + tt - (cw - 1):CONF_HIST + tt]

    ubuf[0:SCONV_HIST] = ubuf[tt:tt + SCONV_HIST]
    cbuf[0:CONF_HIST] = cbuf[tt:tt + CONF_HIST]


def _conv_prompt(z, sconv_w, conf_w, conf_b, ln_g, ln_b, nb, t, d, col0):
    sw, cw = sconv_w.shape[0], conf_w.shape[0]
    tt = _tile(t, 256, CONF_HIST)
    n = t // tt
    row = lambda col: (lambda bi, j: (bi * n + j, col0 + col))
    vec = lambda r: pl.BlockSpec((r, d), lambda bi, j: (0, 0))
    kern = functools.partial(_conv_kernel, tt=tt, sw=sw, cw=cw)
    blk = lambda col: pl.BlockSpec((tt, d), row(col))
    return pl.pallas_call(
        kern, grid=(nb, n),
        in_specs=[blk(0), blk(1), blk(2), blk(3), blk(4), vec(sw), vec(cw), vec(1), vec(1), vec(1)],
        out_specs=[pl.BlockSpec((tt, d), lambda bi, j: (bi * n + j, 0)),
                   pl.BlockSpec((tt, d), lambda bi, j: (bi * n + j, 0)),
                   pl.BlockSpec((1, sw - 1, d), lambda bi, j: (bi, 0, 0)),
                   pl.BlockSpec((1, cw - 1, d), lambda bi, j: (bi, 0, 0))],
        out_shape=[jax.ShapeDtypeStruct((nb * t, d), BF16), jax.ShapeDtypeStruct((nb * t, d), BF16),
                   jax.ShapeDtypeStruct((nb, sw - 1, d), F32), jax.ShapeDtypeStruct((nb, cw - 1, d), F32)],
        scratch_shapes=[pltpu.VMEM((SCONV_HIST + tt, d), F32), pltpu.VMEM((CONF_HIST + tt, d), F32)],
        compiler_params=_params("parallel", "arbitrary"), name="conv_prompt")(
            z, z, z, z, z, sconv_w, conf_w, conf_b, ln_g, ln_b)


def _conv_step_kernel(gb_ref, gc_ref, hb_ref, ga_ref, gs_ref, sc_ref, cc_ref, sw_ref, cw_ref, cb_ref,
                      lg_ref, lb_ref, prev_s_ref, prev_c_ref, ob_ref, oc_ref, ns_ref, nc_ref, *, tb, sw, cw):
    del prev_s_ref, prev_c_ref
    u = gc_ref[...] * hb_ref[...]
    uc = ga_ref[...] * _sigmoid(gs_ref[...])
    row = lax.broadcasted_iota(jnp.int32, u.shape, 0)
    ys = jnp.zeros_like(u)
    yc = jnp.zeros_like(u)
    for n in range(tb):
        ys = jnp.where(row == n, jnp.sum(sc_ref[0, n] * sw_ref[0:sw - 1], axis=0, keepdims=True), ys)
        yc = jnp.where(row == n, jnp.sum(cc_ref[0, n] * cw_ref[0:cw - 1], axis=0, keepdims=True), yc)
        ns_ref[0, n, 0:sw - 2] = sc_ref[0, n, 1:sw - 1]
        ns_ref[0, n, sw - 2:sw - 1] = u[n:n + 1]
        nc_ref[0, n, 0:cw - 2] = cc_ref[0, n, 1:cw - 1]
        nc_ref[0, n, cw - 2:cw - 1] = uc[n:n + 1]
    yb = ys + sw_ref[sw - 1:sw] * u
    ob_ref[...] = (gb_ref[...] * yb).astype(BF16)
    y = yc + cw_ref[cw - 1:cw] * uc + cb_ref[...]
    oc_ref[...] = _silu(_layer_norm(y, lg_ref[...], lb_ref[...])).astype(BF16)


def _conv_step(z, caches_s, caches_c, new_s, new_c, layer, sconv_w, conf_w, conf_b, ln_g, ln_b, d, col0):
    m = z.shape[0]
    sw, cw = sconv_w.shape[0], conf_w.shape[0]
    tb = 16
    blk = lambda col: pl.BlockSpec((tb, d), lambda i: (i, col0 + col))
    vec = lambda r: pl.BlockSpec((r, d), lambda i: (0, 0))
    out = pl.BlockSpec((tb, d), lambda i: (i, 0))
    cs = pl.BlockSpec((1, tb, sw - 1, d), lambda i: (layer, i, 0, 0))
    cc = pl.BlockSpec((1, tb, cw - 1, d), lambda i: (layer, i, 0, 0))
    kern = functools.partial(_conv_step_kernel, tb=tb, sw=sw, cw=cw)
    return pl.pallas_call(
        kern, grid=(m // tb,),
        in_specs=[blk(0), blk(1), blk(2), blk(3), blk(4), cs, cc,
                  vec(sw), vec(cw), vec(1), vec(1), vec(1),
                  pl.BlockSpec(memory_space=pl.ANY), pl.BlockSpec(memory_space=pl.ANY)],
        out_specs=[out, out, cs, cc],
        out_shape=[jax.ShapeDtypeStruct((m, d), BF16), jax.ShapeDtypeStruct((m, d), BF16),
                   jax.ShapeDtypeStruct(caches_s.shape, F32), jax.ShapeDtypeStruct(caches_c.shape, F32)],
        input_output_aliases={12: 2, 13: 3},
        compiler_params=_params("parallel"), name="conv_step")(
            z, z, z, z, z, caches_s, caches_c, sconv_w, conf_w, conf_b, ln_g, ln_b, new_s, new_c)


def _merge_kernel(oa_ref, ob_ref, oc_ref, od_ref, g0_ref, g1_ref, g2_ref, g3_ref, h_ref,
                  wa_ref, wb_ref, wc_ref, wd_ref, wo_ref, lg_ref, lb_ref, prev_ref, o_ref, *, alpha):
    del prev_ref
    merged = _sigmoid(g0_ref[...]) * _dot(oa_ref[...], wa_ref[...])
    merged = merged + _sigmoid(g1_ref[...]) * _dot(ob_ref[...], wb_ref[...])
    merged = merged + _sigmoid(g2_ref[...]) * _dot(oc_ref[...], wc_ref[...])
    merged = merged + _sigmoid(g3_ref[...]) * _dot(od_ref[...], wd_ref[...])
    out = _dot(merged.astype(BF16), wo_ref[...])
    o_ref[...] = _layer_norm(alpha * h_ref[...] + out, lg_ref[...], lb_ref[...])


def _merge(oa, ob, oc, od, z, h, out, wa, wb, wc, wd, wo, ln_g, ln_b, alpha, gate_col0, row0):
    m, d = oa.shape
    tm = _tile(m, 256, 16)
    assert row0 % tm == 0
    r0 = row0 // tm
    row = pl.BlockSpec((tm, d), lambda i: (i, 0))
    mrow = pl.BlockSpec((tm, d), lambda i: (r0 + i, 0))
    gate = lambda c: pl.BlockSpec((tm, d), lambda i: (i, gate_col0 + c))
    wsp = pl.BlockSpec((d, d), lambda i: (0, 0))
    vec = pl.BlockSpec((1, d), lambda i: (0, 0))
    return pl.pallas_call(
        functools.partial(_merge_kernel, alpha=alpha), grid=(m // tm,),
        in_specs=[row, row, row, row, gate(0), gate(1), gate(2), gate(3), mrow, wsp, wsp, wsp, wsp, wsp, vec, vec,
                  pl.BlockSpec(memory_space=pl.ANY)],
        out_specs=mrow, out_shape=jax.ShapeDtypeStruct(out.shape, F32),
        input_output_aliases={16: 0},
        compiler_params=_params("parallel"), name="merge")(
            oa, ob, oc, od, z, z, z, z, h, wa, wb, wc, wd, wo, ln_g, ln_b, out)


def _router_kernel(x_ref, w_ref, b_ref, wt_ref, pos_ref, len_ref, off_ref, base_ref, tot_ref, carry_ref,
                   *, n_exp):
    i = pl.program_id(0)

    @pl.when(i == 0)
    def _():
        carry_ref[...] = jnp.zeros_like(carry_ref)

    tm = x_ref.shape[0]
    logits = _dot(x_ref[...].astype(BF16), w_ref[...].astype(BF16)) + b_ref[...]
    lane = lax.broadcasted_iota(jnp.int32, (tm, LANES), 1)
    lane_f = lane.astype(F32)
    work = jnp.where(lane < n_exp, logits, -jnp.inf)
    vals, hots = [], []
    for kk in range(TOP_K):
        m = jnp.max(work, axis=-1, keepdims=True)
        first = jnp.min(jnp.where(work == m, lane_f, float(LANES)), axis=-1, keepdims=True)
        hot = lane_f == first
        work = jnp.where(hot, -jnp.inf, work)
        vals.append(m)
        hots.append(hot)
    es = [jnp.exp(v - vals[0]) for v in vals]
    den = es[0]
    for e in es[1:]:
        den = den + e
    wt = jnp.zeros((tm, LANES), F32)
    for kk in range(TOP_K):
        wt = jnp.where(lane == kk, es[kk] / den, wt)
    chosen = hots[0]
    for hot in hots[1:]:
        chosen = chosen | hot
    onehot = jnp.where(chosen, 1.0, 0.0)
    r = lax.broadcasted_iota(jnp.int32, (tm, tm), 0)
    c = lax.broadcasted_iota(jnp.int32, (tm, tm), 1)
    before = jnp.where(c < r, 1.0, 0.0).astype(BF16)
    seen = _dot(before, onehot.astype(BF16))
    cnt = jnp.sum(onehot, axis=0, keepdims=True)
    groups = jnp.floor((cnt + (SUBLANES - 1)) * (1.0 / SUBLANES))
    er = lax.broadcasted_iota(jnp.int32, (LANES, LANES), 0)
    ec = lax.broadcasted_iota(jnp.int32, (LANES, LANES), 1)
    earlier = jnp.where(er < ec, 1.0, 0.0).astype(BF16)
    groups8 = jnp.broadcast_to(groups, (SUBLANES, LANES)).astype(BF16)
    toff = _dot(groups8, earlier)[0:1] * float(SUBLANES)
    cnt_pad = groups * float(SUBLANES)
    where_row = toff + seen
    pos = jnp.zeros((tm, LANES), F32)
    for kk in range(TOP_K):
        pk = jnp.sum(jnp.where(hots[kk], where_row, 0.0), axis=-1, keepdims=True)
        pos = jnp.where(lane == kk, pk, pos)
    wt_ref[...] = wt
    pos_ref[...] = pos.astype(jnp.int32)
    len_ref[0] = cnt_pad.astype(jnp.int32)
    off_ref[0] = toff.astype(jnp.int32)
    base_ref[0] = carry_ref[...].astype(jnp.int32)
    carry_ref[...] = carry_ref[...] + cnt_pad
    tot_ref[...] = carry_ref[...].astype(jnp.int32)


def _router(x, w_pad, b_pad, n_exp, tm):
    m, d = x.shape
    row = pl.BlockSpec((tm, LANES), lambda i: (i, 0))
    one = pl.BlockSpec((1, LANES), lambda i: (0, 0))
    meta = pl.BlockSpec((1, 1, LANES), lambda i: (i, 0, 0))
    n = m // tm
    meta_shape = jax.ShapeDtypeStruct((n, 1, LANES), jnp.int32)
    return pl.pallas_call(
        functools.partial(_router_kernel, n_exp=n_exp), grid=(n,),
        in_specs=[pl.BlockSpec((tm, d), lambda i: (i, 0)), pl.BlockSpec((d, LANES), lambda i: (0, 0)), one],
        out_specs=[row, row, meta, meta, meta, one],
        out_shape=[jax.ShapeDtypeStruct((m, LANES), F32), jax.ShapeDtypeStruct((m, LANES), jnp.int32),
                   meta_shape, meta_shape, meta_shape, jax.ShapeDtypeStruct((1, LANES), jnp.int32)],
        scratch_shapes=[pltpu.VMEM((1, LANES), F32)],
        compiler_params=_params("arbitrary"), name="router")(x, w_pad, b_pad)


def _dispatch_kernel(pend_ref, len_ref, off_ref, dst_ref, plen_ref, poff_ref, pdst_ref, x_ref, pos_ref, xs_ref,
                     zbuf, grouped, sem, zsem, *, n_exp, sizes):
    @pl.when(pl.program_id(0) == 0)
    def _():
        zbuf[...] = jnp.zeros_like(zbuf)

        def zero_copy(e):
            first = pl.multiple_of(pend_ref[e] - MOE_ROWS, MOE_ROWS)
            return pltpu.make_async_copy(zbuf, xs_ref.at[pl.ds(first, MOE_ROWS)], zsem)

        def nonempty(e):
            return pend_ref[e] > jnp.where(e > 0, pend_ref[jnp.maximum(e - 1, 0)], 0)

        def zstart(e, carry):
            @pl.when(nonempty(e))
            def _():
                zero_copy(e).start()
            return carry

        def zwait(e, carry):
            @pl.when(nonempty(e))
            def _():
                zero_copy(e).wait()
            return carry

        lax.fori_loop(0, n_exp, zstart, 0)
        lax.fori_loop(0, n_exp, zwait, 0)

        def tail_copy(b):
            return pltpu.make_async_copy(zbuf, xs_ref.at[pl.ds(pl.multiple_of(b * MOE_ROWS, MOE_ROWS), MOE_ROWS)], zsem)

        def tstart(b, carry):
            tail_copy(b).start()
            return carry

        def twait(b, carry):
            tail_copy(b).wait()
            return carry

        first_unused = pend_ref[n_exp - 1] // MOE_ROWS
        lax.fori_loop(first_unused, xs_ref.shape[0] // MOE_ROWS, tstart, 0)
        lax.fori_loop(first_unused, xs_ref.shape[0] // MOE_ROWS, twait, 0)

    i = pl.program_id(0)
    buf = i % 2
    tt = x_ref.shape[0]
    cap = grouped.shape[1]
    slot = lax.broadcasted_iota(jnp.int32, (tt, cap), 1)
    pos = pos_ref[...]
    placed = slot == pos[:, 0:1]
    for kk in range(1, TOP_K):
        placed = placed | (slot == pos[:, kk:kk + 1])
    grouped[buf] = _dot_tn(jnp.where(placed, 1.0, 0.0).astype(BF16), x_ref[...].astype(BF16))

    def piece(meta, b):
        lens, offs, dsts = meta

        def make(e, offset, size):
            src = pl.multiple_of(offs[e] + offset, SUBLANES)
            dst = pl.multiple_of(dsts[e] + offset, SUBLANES)
            return pltpu.make_async_copy(grouped.at[b, pl.ds(src, size)], xs_ref.at[pl.ds(dst, size)], sem.at[b])
        return lens, make

    @pl.when(i > 0)
    def _():
        lens, make = piece((plen_ref, poff_ref, pdst_ref), 1 - buf)
        _for_each_run_piece(lens, n_exp, sizes, lambda *a: make(*a).wait())

    lens, make = piece((len_ref, off_ref, dst_ref), buf)
    _for_each_run_piece(lens, n_exp, sizes, lambda *a: make(*a).start())

    @pl.when(i == pl.num_programs(0) - 1)
    def _():
        _for_each_run_piece(lens, n_exp, sizes, lambda *a: make(*a).wait())


def _run_sizes(tt):
    sizes = [SUBLANES]
    while sizes[-1] * 2 <= tt + SUBLANES - 1:
        sizes.append(sizes[-1] * 2)
    return tuple(reversed(sizes))


def _for_each_run_piece(len_ref, n_exp, sizes, fn):
    def body(e, carry):
        length = len_ref[e]
        for size in sizes:
            @pl.when((length & size) != 0)
            def _():
                fn(e, length & ~(2 * size - 1), size)
        return carry

    lax.fori_loop(0, n_exp, body, 0)


def _grouped_rows(tt, n_exp):
    cap = tt * TOP_K + n_exp * SUBLANES
    return -(-cap // LANES) * LANES


def _dispatch(x, pos, run_len, run_off, run_dst, p_end, rows, tt):
    m, d = x.shape
    n_exp = p_end.shape[0]
    meta = pl.BlockSpec((LANES,), lambda i, pe: (i,), memory_space=pltpu.SMEM)
    prev = pl.BlockSpec((LANES,), lambda i, pe: (jnp.maximum(i - 1, 0),), memory_space=pltpu.SMEM)
    grid_spec = pltpu.PrefetchScalarGridSpec(
        num_scalar_prefetch=1, grid=(m // tt,),
        in_specs=[meta, meta, meta, prev, prev, prev,
                  pl.BlockSpec((tt, d), lambda i, pe: (i, 0)),
                  pl.BlockSpec((tt, LANES), lambda i, pe: (i, 0))],
        out_specs=pl.BlockSpec(memory_space=pl.ANY),
        scratch_shapes=[pltpu.VMEM((MOE_ROWS, d), F32), pltpu.VMEM((2, _grouped_rows(tt, n_exp), d), F32),
                        pltpu.SemaphoreType.DMA((2,)), pltpu.SemaphoreType.DMA(())])
    return pl.pallas_call(
        functools.partial(_dispatch_kernel, n_exp=n_exp, sizes=_run_sizes(tt)), grid_spec=grid_spec,
        out_shape=jax.ShapeDtypeStruct((rows, d), F32),
        compiler_params=_params("arbitrary"), name="moe_dispatch")(
            p_end, run_len, run_off, run_dst, run_len, run_off, run_dst, x, pos)


def _expert_kernel(be_ref, nu_ref, xs_ref, wgu_ref, bgu_ref, wd_ref, bd_ref, y_ref, wgu_bf, wd_bf, *, d_ff):
    i = pl.program_id(0)
    prev = be_ref[jnp.maximum(i - 1, 0)]

    @pl.when((i == 0) | (be_ref[i] != prev))
    def _():
        wgu_bf[...] = wgu_ref[0, 0].astype(BF16)
        wd_bf[...] = wd_ref[0, 0].astype(BF16)

    @pl.when(i < nu_ref[0])
    def _():
        gu = _dot(xs_ref[...].astype(BF16), wgu_bf[...]) + bgu_ref[0]
        x_glu = jnp.minimum(gu[:, :d_ff], SWIGLU_LIMIT)
        x_lin = jnp.clip(gu[:, d_ff:], -SWIGLU_LIMIT, SWIGLU_LIMIT)
        act = x_glu * _sigmoid(SWIGLU_ALPHA * x_glu) * (x_lin + 1.0)
        y_ref[...] = _dot(act.astype(BF16), wd_bf[...]) + bd_ref[0]

    @pl.when(i >= nu_ref[0])
    def _():
        y_ref[...] = jnp.zeros_like(y_ref)


def _experts(xs, block_expert, n_used, w_gate_up, b_gate_up, w_down, b_down, layer):
    rows, d = xs.shape
    _, n_exp, _, ff2 = w_gate_up.shape
    d_ff = ff2 // 2
    n_blocks = rows // MOE_ROWS
    blk = lambda i, be, nu: (jnp.minimum(i, nu[0] - 1), 0)
    grid_spec = pltpu.PrefetchScalarGridSpec(
        num_scalar_prefetch=2, grid=(n_blocks,),
        in_specs=[pl.BlockSpec((MOE_ROWS, d), blk),
                  pl.BlockSpec((1, 1, d, ff2), lambda i, be, nu: (layer, be[i], 0, 0)),
                  pl.BlockSpec((1, 1, ff2), lambda i, be, nu: (be[i], 0, 0)),
                  pl.BlockSpec((1, 1, d_ff, d), lambda i, be, nu: (layer, be[i], 0, 0)),
                  pl.BlockSpec((1, 1, d), lambda i, be, nu: (be[i], 0, 0))],
        out_specs=pl.BlockSpec((MOE_ROWS, d), lambda i, be, nu: (i, 0)),
        scratch_shapes=[pltpu.VMEM((d, ff2), BF16), pltpu.VMEM((d_ff, d), BF16)])
    return pl.pallas_call(
        functools.partial(_expert_kernel, d_ff=d_ff), grid_spec=grid_spec,
        out_shape=jax.ShapeDtypeStruct((rows, d), F32),
        compiler_params=_params("arbitrary"), name="moe_experts")(
            block_expert, n_used, xs, w_gate_up, b_gate_up.reshape(n_exp, 1, ff2),
            w_down, b_down.reshape(n_exp, 1, d))


def _combine_kernel(len_ref, off_ref, dst_ref, nlen_ref, noff_ref, ndst_ref, pos_ref, wt_ref, h_ref, lg_ref, lb_ref,
                    y_ref, o_ref, ob_ref, rows, moe, sem, *, n_exp, sizes, alpha):
    i = pl.program_id(0)
    buf = i % 2

    def piece(meta, b):
        lens, offs, dsts = meta

        def make(e, offset, size):
            src = pl.multiple_of(dsts[e] + offset, SUBLANES)
            dst = pl.multiple_of(offs[e] + offset, SUBLANES)
            return pltpu.make_async_copy(y_ref.at[pl.ds(src, size)], rows.at[b, pl.ds(dst, size)], sem.at[b])
        return lens, make

    lens, make = piece((len_ref, off_ref, dst_ref), buf)

    @pl.when(i == 0)
    def _():
        _for_each_run_piece(lens, n_exp, sizes, lambda *a: make(*a).start())

    @pl.when(i + 1 < pl.num_programs(0))
    def _():
        nlens, nmake = piece((nlen_ref, noff_ref, ndst_ref), 1 - buf)
        _for_each_run_piece(nlens, n_exp, sizes, lambda *a: nmake(*a).start())

    _for_each_run_piece(lens, n_exp, sizes, lambda *a: make(*a).wait())

    def token(n, carry):
        acc = wt_ref[n * TOP_K] * rows[buf, pl.ds(pos_ref[n * TOP_K], 1), :]
        for kk in range(1, TOP_K):
            acc = acc + wt_ref[n * TOP_K + kk] * rows[buf, pl.ds(pos_ref[n * TOP_K + kk], 1), :]
        moe[pl.ds(n, 1), :] = acc
        return carry

    lax.fori_loop(0, h_ref.shape[0], token, 0, unroll=2)
    y = _layer_norm(alpha * h_ref[...] + moe[...], lg_ref[...], lb_ref[...])
    o_ref[...] = y
    ob_ref[...] = y.astype(BF16)


def _combine(y, pos_flat, wt_flat, run_len, run_off, run_dst, h, ln_g, ln_b, alpha, n_exp, tt):
    m, d = h.shape
    vec = pl.BlockSpec((1, d), lambda i: (0, 0))
    row = pl.BlockSpec((tt, d), lambda i: (i, 0))
    n_tiles = m // tt
    meta = pl.BlockSpec((LANES,), lambda i: (i,), memory_space=pltpu.SMEM)
    nxt = pl.BlockSpec((LANES,), lambda i: (jnp.minimum(i + 1, n_tiles - 1),), memory_space=pltpu.SMEM)
    per_tok = pl.BlockSpec((pos_flat.shape[0] // n_tiles,), lambda i: (i,), memory_space=pltpu.SMEM)
    return pl.pallas_call(
        functools.partial(_combine_kernel, n_exp=n_exp, sizes=_run_sizes(tt), alpha=alpha), grid=(n_tiles,),
        in_specs=[meta, meta, meta, nxt, nxt, nxt, per_tok, per_tok, row, vec, vec,
                  pl.BlockSpec(memory_space=pl.ANY)],
        out_specs=[row, row],
        out_shape=[jax.ShapeDtypeStruct((m, d), F32), jax.ShapeDtypeStruct((m, d), BF16)],
        scratch_shapes=[pltpu.VMEM((2, _grouped_rows(tt, n_exp), d), F32), pltpu.VMEM((tt, d), F32),
                        pltpu.SemaphoreType.DMA((2,))],
        compiler_params=_params("arbitrary"), name="moe_combine")(
            run_len, run_off, run_dst, run_len, run_off, run_dst, pos_flat, wt_flat, h, ln_g, ln_b, y)


def _moe(h, w, alpha, layer):
    m, d = h.shape
    n_exp = w["n_exp"]
    tt = _tile(m, 384, 16)
    n_tiles = m // tt
    wt, pos, run_len, run_off, run_base, totals = _router(h, w["w_router_pad"], w["b_router_pad"], n_exp, tt)
    seg = totals[0, :n_exp]
    padded = (seg + MOE_ROWS - 1) // MOE_ROWS * MOE_ROWS
    p_end = jnp.cumsum(padded).astype(jnp.int32)
    p_start = jnp.zeros((LANES,), jnp.int32).at[:n_exp].set(p_end - padded)
    run_dst = (run_base[:, 0, :] + p_start[None, :]).reshape(-1)
    run_len, run_off = run_len.reshape(-1), run_off.reshape(-1)
    n_blocks = -(-(m * TOP_K + n_tiles * n_exp * (SUBLANES - 1)) // MOE_ROWS) + n_exp
    block_start = jnp.arange(n_blocks, dtype=jnp.int32) * MOE_ROWS
    block_expert = jnp.minimum(jnp.sum(p_end[None, :] <= block_start[:, None], axis=1), n_exp - 1).astype(jnp.int32)
    n_used = p_end[-1:] // MOE_ROWS
    xs = _dispatch(h, pos, run_len, run_off, run_dst, p_end, n_blocks * MOE_ROWS, tt)
    y = _experts(xs, block_expert, n_used, w["w_gate_up"], w["b_gate_up"], w["w_down"], w["b_down"], layer)
    chunk = max(LANES, 1 << (tt * TOP_K - 1).bit_length())
    per_tile = lambda a: jnp.pad(a[:, :TOP_K].reshape(n_tiles, tt * TOP_K),
                                 ((0, 0), (0, chunk - tt * TOP_K))).reshape(-1)
    return _combine(y, per_tile(pos), per_tile(wt), run_len, run_off, run_dst,
                    h, w["ln2_g"], w["ln2_b"], alpha, n_exp, tt)


def _mixers(h, hb, h1, w, dims, alpha, states, layer, row0):
    nb, t, d, gh, gdk, gdv, hh, hdk, hdv = dims
    z = _matmul(hb, w["w_in_main"], layer, row0, nb * t)
    la = _gla_decay(hb, w["w_lra"], w["w_gla_lr"], w["b_gla_lr"], row0, nb * t)
    col_conv = (2 * gh * gdk + 2 * gh * gdv) // d
    col_hgrn = col_conv + 5
    col_gate = col_hgrn + 4
    if states is None:
        oa, s_gla = _gla_prompt(z, la, w["gla_norm_g"], nb, t, gh, gdk, gdv)
        od, s_hgrn = _hgrn_prompt(z, w["hgrn_lb"], w["hgrn_norm_g"], nb, t, hh, hdk, hdv, col_hgrn)
        ob, oc, c_s, c_c = _conv_prompt(z, w["sconv_w"], w["conf_conv_w"], w["conf_conv_b"],
                                        w["conf_ln_g"], w["conf_ln_b"], nb, t, d, col_conv)
    else:
        st_gla, st_hgrn, new_gla, new_hgrn, caches_s, caches_c, new_s, new_c = states
        oa, s_gla = _gla_step(z, la, w["gla_norm_g"], st_gla, new_gla, layer)
        od, s_hgrn = _hgrn_step(z, w["hgrn_lb"], w["hgrn_norm_g"], st_hgrn, new_hgrn, layer, col_hgrn)
        ob, oc, c_s, c_c = _conv_step(z, caches_s, caches_c, new_s, new_c, layer, w["sconv_w"], w["conf_conv_w"],
                                      w["conf_conv_b"], w["conf_ln_g"], w["conf_ln_b"], d, col_conv)
    h1 = _merge(oa, ob, oc, od, z, h, h1, w["w_br_a"], w["w_br_b"], w["w_br_c"], w["w_br_d"], w["w_o"],
                w["ln1_g"], w["ln1_b"], alpha, col_gate, row0)
    return h1, (s_gla, s_hgrn, c_s, c_c)


def kernel(x_prompt, x_sample, state_gla, state_hgrn, cache_sconv, cache_conformer, ln_in_g, ln_in_b, w_in, w_gla_lr, b_gla_lr, gla_norm_g, w_br_a, sconv_w, w_br_b, conf_conv_w, conf_conv_b, conf_ln_g, conf_ln_b, w_br_c, hgrn_lb_logits, hgrn_norm_g, w_br_d, w_o, ln1_g, ln1_b, w_router, b_router, w_gate_up, b_gate_up, w_down, b_down, ln2_g, ln2_b):
    depth, d, _ = w_in.shape
    _, _, gh, gdk, gdv = state_gla.shape
    _, _, hh, hdk, hdv = state_hgrn.shape
    rank = w_gla_lr.shape[1]
    n_exp = w_router.shape[2]
    alpha = (2 * depth) ** 0.25
    lra0 = 2 * gh * gdk + 2 * gh * gdv

    lb_p = jax.nn.softmax(hgrn_lb_logits.astype(F32), axis=0)
    hgrn_lb = jnp.cumsum(lb_p, axis=0) - lb_p[:1]
    row = lambda a: a.reshape(1, -1)
    w_in_main = jnp.concatenate([w_in[:, :, :lra0], w_in[:, :, lra0 + rank:]], axis=2).astype(BF16)
    layers = []
    for l in range(depth):
        w_lra = jnp.zeros((d, LANES), F32).at[:, :rank].set(w_in[l, :, lra0:lra0 + rank])
        w2 = jnp.zeros((LANES, gh * gdk), F32).at[:rank].set(w_gla_lr[l])
        layers.append({
            "w_in_main": w_in_main,
            "w_lra": w_lra.astype(BF16), "w_gla_lr": w2.astype(BF16), "b_gla_lr": row(b_gla_lr[l]),
            "gla_norm_g": row(gla_norm_g[l]), "hgrn_norm_g": row(hgrn_norm_g[l]), "hgrn_lb": row(hgrn_lb[l]),
            "sconv_w": sconv_w[l], "conf_conv_w": conf_conv_w[l], "conf_conv_b": row(conf_conv_b[l]),
            "conf_ln_g": row(conf_ln_g[l]), "conf_ln_b": row(conf_ln_b[l]),
            "w_br_a": w_br_a[l].astype(BF16), "w_br_b": w_br_b[l].astype(BF16),
            "w_br_c": w_br_c[l].astype(BF16), "w_br_d": w_br_d[l].astype(BF16), "w_o": w_o[l].astype(BF16),
            "ln1_g": row(ln1_g[l]), "ln1_b": row(ln1_b[l]), "ln2_g": row(ln2_g[l]), "ln2_b": row(ln2_b[l]),
            "n_exp": n_exp,
            "w_router_pad": jnp.zeros((d, LANES), F32).at[:, :n_exp].set(w_router[l]),
            "b_router_pad": jnp.zeros((1, LANES), F32).at[0, :n_exp].set(b_router[l]),
            "w_gate_up": w_gate_up, "b_gate_up": b_gate_up[l], "w_down": w_down, "b_down": b_down[l],
        })

    nbp, tp, _ = x_prompt.shape
    nbs, ts, _ = x_sample.shape
    assert ts == 1, "the sample group advances its states by exactly one token"
    dims_p = (nbp, tp, d, gh, gdk, gdv, hh, hdk, hdv)
    dims_s = (nbs, ts, d, gh, gdk, gdv, hh, hdk, hdv)
    mp, ms = nbp * tp, nbs * ts
    h, hb = _ln_call(jnp.concatenate([x_prompt.reshape(mp, d), x_sample.reshape(ms, d)], axis=0), ln_in_g, ln_in_b)
    prompt_states = ([], [], [], [])
    new_gla, new_hgrn = jnp.zeros_like(state_gla), jnp.zeros_like(state_hgrn)
    new_s, new_c = jnp.zeros_like(cache_sconv), jnp.zeros_like(cache_conformer)
    for l, w in enumerate(layers):
        h1 = jnp.zeros_like(h)
        h1, st_p = _mixers(h, hb, h1, w, dims_p, alpha, None, l, 0)
        h1, (new_gla, new_hgrn, new_s, new_c) = _mixers(
            h, hb, h1, w, dims_s, alpha,
            (state_gla, state_hgrn, new_gla, new_hgrn, cache_sconv, cache_conformer, new_s, new_c), l, mp)
        h, hb = _moe(h1, w, alpha, l)
        for acc, s in zip(prompt_states, st_p):
            acc.append(s)
    y_p, y_s = h[:mp].reshape(nbp, tp, d), h[mp:].reshape(nbs, ts, d)
    return ((y_p, y_s) + tuple(jnp.stack(a) for a in prompt_states) + (new_gla, new_hgrn, new_s, new_c))
```

```python
import functools

import jax
import jax.numpy as jnp
from jax import lax
from jax.experimental import pallas as pl
from jax.experimental.pallas import tpu as pltpu

F32 = jnp.float32
BF16 = jnp.bfloat16

TOP_K = 4
CHUNK = 64
SUB = 16
GLA_GATE_NORM = 16.0
F_MIN = 1e-20
LN_EPS = 1e-5
RMS_EPS = 1e-6
SWIGLU_ALPHA = 1.702
SWIGLU_LIMIT = 7.0
LOG2_E = 1.4426950408889634
MOE_ROWS = 512
LANES = 128
SUBLANES = 8
CONV_ROWS = 32
CONF_HIST = 32
SCONV_HIST = 8
V7X_VMEM_LIMIT = 56 * 1024 * 1024


def _params(*sem):
    return pltpu.CompilerParams(dimension_semantics=sem, vmem_limit_bytes=V7X_VMEM_LIMIT)


def _tile(n, pref, mult=8):
    if n <= pref:
        return n
    t = pref - pref % mult
    while t >= mult:
        if n % t == 0:
            return t
        t -= mult
    raise ValueError(f"no tile for {n}")


def _sigmoid(x):
    return 1.0 / (1.0 + jnp.exp(-x))


def _silu(x):
    return x * _sigmoid(x)


def _log_sigmoid(x):
    return jnp.minimum(x, 0.0) - jnp.log(1.0 + jnp.exp(-jnp.abs(x)))


def _layer_norm(x, g, b):
    mu = jnp.mean(x, axis=-1, keepdims=True)
    xc = x - mu
    var = jnp.mean(xc * xc, axis=-1, keepdims=True)
    return xc * lax.rsqrt(var + LN_EPS) * g + b


def _dot(a, b):
    return jnp.dot(a, b, preferred_element_type=F32)


def _dot_nt(a, b):
    return lax.dot_general(a, b, (((1,), (1,)), ((), ())), preferred_element_type=F32)


def _dot_tn(a, b):
    return lax.dot_general(a, b, (((0,), (0,)), ((), ())), preferred_element_type=F32)


def _split3(x):
    hi = x.astype(BF16)
    r = x - hi.astype(F32)
    mid = r.astype(BF16)
    lo = (r - mid.astype(F32)).astype(BF16)
    return hi, mid, lo


def _ln_kernel(x_ref, g_ref, b_ref, o_ref, ob_ref):
    y = _layer_norm(x_ref[...], g_ref[...], b_ref[...])
    o_ref[...] = y
    ob_ref[...] = y.astype(BF16)


def _ln_call(x, g, b):
    m, d = x.shape
    tm = _tile(m, 512, 16)
    row = pl.BlockSpec((tm, d), lambda i: (i, 0))
    vec = pl.BlockSpec((1, d), lambda i: (0, 0))
    return pl.pallas_call(
        _ln_kernel, grid=(m // tm,), in_specs=[row, vec, vec], out_specs=[row, row],
        out_shape=[jax.ShapeDtypeStruct((m, d), F32), jax.ShapeDtypeStruct((m, d), BF16)],
        compiler_params=_params("parallel"), name="ln_in")(x, g.reshape(1, d), b.reshape(1, d))


def _mm_kernel(x_ref, w_ref, o_ref):
    o_ref[...] = _dot(x_ref[...], w_ref[0])


def _matmul(x, w, layer, row0, m):
    k = x.shape[1]
    n = w.shape[2]
    tm = _tile(m, 1024, 16)
    tn = _tile(n, 2048, LANES)
    assert row0 % tm == 0
    r0 = row0 // tm
    return pl.pallas_call(
        _mm_kernel, grid=(n // tn, m // tm),
        in_specs=[pl.BlockSpec((tm, k), lambda j, i: (r0 + i, 0)),
                  pl.BlockSpec((1, k, tn), lambda j, i: (layer, 0, j))],
        out_specs=pl.BlockSpec((tm, tn), lambda j, i: (i, j)),
        out_shape=jax.ShapeDtypeStruct((m, n), F32),
        compiler_params=_params("parallel", "parallel"), name="in_proj")(x, w)


def _gla_decay_kernel(h_ref, w1_ref, w2_ref, b2_ref, o_ref):
    lra = _dot(h_ref[...], w1_ref[...])
    x = _dot(lra.astype(BF16), w2_ref[...]) + b2_ref[...]
    o_ref[...] = _log_sigmoid(x) * (1.0 / GLA_GATE_NORM)


def _gla_decay(hb, w1, w2, b2, row0, m):
    d = hb.shape[1]
    n = w2.shape[1]
    tm = _tile(m, 512, 16)
    assert row0 % tm == 0
    r0 = row0 // tm
    return pl.pallas_call(
        _gla_decay_kernel, grid=(m // tm,),
        in_specs=[pl.BlockSpec((tm, d), lambda i: (r0 + i, 0)), pl.BlockSpec(w1.shape, lambda i: (0, 0)),
                  pl.BlockSpec(w2.shape, lambda i: (0, 0)), pl.BlockSpec((1, n), lambda i: (0, 0))],
        out_specs=pl.BlockSpec((tm, n), lambda i: (i, 0)),
        out_shape=jax.ShapeDtypeStruct((m, n), F32),
        compiler_params=_params("parallel"), name="gla_decay")(hb, w1, w2, b2)


def _cumsum_rows(x):
    c = x.shape[0]
    r = lax.broadcasted_iota(jnp.int32, (c, c), 0)
    s = lax.broadcasted_iota(jnp.int32, (c, c), 1)
    tri = jnp.where(r >= s, 1.0, 0.0).astype(BF16)
    hi, mid, lo = _split3(x)
    return _dot(tri, hi) + _dot(tri, mid) + _dot(tri, lo)


def _intra_scores(q, k, b2):
    c, dk = q.shape
    half = SUB // 2
    lane = lax.broadcasted_iota(jnp.int32, (half, c), 1)
    row = lax.broadcasted_iota(jnp.int32, (half, c), 0)
    blocks = []
    for i in range(c // SUB):
        lo = i * SUB
        qi, ki, bi = q[lo:lo + SUB], k[lo:lo + SUB], b2[lo:lo + SUB]
        if i > 0:
            ref = b2[lo - 1:lo]
            qs = qi * jnp.exp2(bi - ref)
            ks = jnp.concatenate([k[:lo] * jnp.exp2(ref - b2[:lo]), jnp.zeros((c - lo, dk), F32)], axis=0)
            a = _dot_nt(qs.astype(BF16), ks.astype(BF16))
            top, bot = a[:half], a[half:]
        else:
            top = bot = jnp.zeros((half, c), F32)
        for s in range(SUB):
            ks_row, bs_row = ki[s:s + 1], bi[s:s + 1]
            if s < half:
                p = qi[:half] * ks_row * jnp.exp2(bi[:half] - bs_row)
                top = jnp.where((lane == lo + s) & (row >= s), jnp.sum(p, axis=-1, keepdims=True), top)
            p = qi[half:] * ks_row * jnp.exp2(bi[half:] - bs_row)
            keep = (lane == lo + s) if s < half else (lane == lo + s) & (row >= s - half)
            bot = jnp.where(keep, jnp.sum(p, axis=-1, keepdims=True), bot)
        blocks += [top, bot]
    return jnp.concatenate(blocks, axis=0)


def _chunk_step(q, k, v, b, st):
    c = q.shape[0]
    b2 = b * LOG2_E
    b_end = b2[c - 1:c]
    o = _dot_nt((q * jnp.exp2(b2)).astype(BF16), st.astype(BF16))
    a = _intra_scores(q, k, b2)
    vb = v.astype(BF16)
    o = o + _dot(a.astype(BF16), vb)
    kb = (k * jnp.exp2(b_end - b2)).astype(BF16)
    st_new = st * jnp.exp2(b_end) + _dot_tn(vb, kb)
    return o, st_new


def _gated_rms(o, gate, g):
    o = o * lax.rsqrt(jnp.mean(o * o, axis=-1, keepdims=True) + RMS_EPS) * g
    return o * _silu(gate)


def _gla_kernel(q_ref, k_ref, v_ref, g_ref, la_ref, ng_ref, o_ref, so_ref, st_ref, *, heads, dk, dv):
    j = pl.program_id(1)

    @pl.when(j == 0)
    def _():
        st_ref[...] = jnp.zeros_like(st_ref)

    scale = dk ** -0.5
    for s in range(q_ref.shape[0]):
        b_all = _cumsum_rows(la_ref[s])
        for h in range(heads):
            ks = slice(h * dk, (h + 1) * dk)
            vs = slice(h * dv, (h + 1) * dv)
            o, st = _chunk_step(q_ref[s, :, ks] * scale, k_ref[s, :, ks], v_ref[s, :, vs], b_all[:, ks],
                                st_ref[s, h])
            st_ref[s, h] = st
            o_ref[s, :, vs] = _gated_rms(o, g_ref[s, :, vs], ng_ref[...]).astype(BF16)

    @pl.when(j == pl.num_programs(1) - 1)
    def _():
        for s in range(q_ref.shape[0]):
            for h in range(heads):
                so_ref[s, h] = st_ref[s, h].T


def _seq_group(nb):
    return next(g for g in (8, 4, 2, 1) if nb % g == 0)


def _gla_prompt(z, la, norm_g, nb, t, heads, dk, dv):
    c = min(CHUNK, t)
    n = t // c
    grp = _seq_group(nb)
    wk, wv = heads * dk, heads * dv
    z3, la3 = z.reshape(nb, t, z.shape[1]), la.reshape(nb, t, la.shape[1])
    row = lambda col: (lambda bi, j: (bi, j, col))
    kern = functools.partial(_gla_kernel, heads=heads, dk=dk, dv=dv)
    o, st = pl.pallas_call(
        kern, grid=(nb // grp, n),
        in_specs=[pl.BlockSpec((grp, c, wk), row(0)), pl.BlockSpec((grp, c, wk), row(1)),
                  pl.BlockSpec((grp, c, wv), row(1)), pl.BlockSpec((grp, c, wv), row(2)),
                  pl.BlockSpec((grp, c, wk), row(0)), pl.BlockSpec((1, dv), lambda bi, j: (0, 0))],
        out_specs=[pl.BlockSpec((grp, c, wv), row(0)),
                   pl.BlockSpec((grp, heads, dk, dv), lambda bi, j: (bi, 0, 0, 0))],
        out_shape=[jax.ShapeDtypeStruct((nb, t, wv), BF16),
                   jax.ShapeDtypeStruct((nb, heads, dk, dv), F32)],
        scratch_shapes=[pltpu.VMEM((grp, heads, dv, dk), F32)],
        compiler_params=_params("parallel", "arbitrary"), name="gla_prompt")(z3, z3, z3, z3, la3, norm_g)
    return o.reshape(nb * t, wv), st


def _hgrn_inputs(qd, fd, lb):
    f = lb + (1.0 - lb) * _sigmoid(fd)
    return _silu(qd), 1.0 - f, jnp.log(jnp.maximum(f, F_MIN))


def _hgrn_kernel(q_ref, f_ref, v_ref, g_ref, lb_ref, ng_ref, o_ref, so_ref, st_ref, *, heads, dk, dv):
    j = pl.program_id(1)

    @pl.when(j == 0)
    def _():
        st_ref[...] = jnp.zeros_like(st_ref)

    for s in range(q_ref.shape[0]):
        q_all, k_all, lf = _hgrn_inputs(q_ref[s], f_ref[s], lb_ref[...])
        b_all = _cumsum_rows(lf)
        for h in range(heads):
            ks = slice(h * dk, (h + 1) * dk)
            vs = slice(h * dv, (h + 1) * dv)
            o, st = _chunk_step(q_all[:, ks], k_all[:, ks], v_ref[s, :, vs], b_all[:, ks], st_ref[s, h])
            st_ref[s, h] = st
            o_ref[s, :, vs] = _gated_rms(o, g_ref[s, :, vs], ng_ref[...]).astype(BF16)

    @pl.when(j == pl.num_programs(1) - 1)
    def _():
        for s in range(q_ref.shape[0]):
            for h in range(heads):
                so_ref[s, h] = st_ref[s, h].T


def _hgrn_prompt(z, lb, norm_g, nb, t, heads, dk, dv, col0):
    c = min(CHUNK, t)
    n = t // c
    grp = _seq_group(nb)
    w = heads * dk
    z3 = z.reshape(nb, t, z.shape[1])
    row = lambda col: (lambda bi, j: (bi, j, col0 + col))
    kern = functools.partial(_hgrn_kernel, heads=heads, dk=dk, dv=dv)
    o, st = pl.pallas_call(
        kern, grid=(nb // grp, n),
        in_specs=[pl.BlockSpec((grp, c, w), row(0)), pl.BlockSpec((grp, c, w), row(1)),
                  pl.BlockSpec((grp, c, w), row(2)), pl.BlockSpec((grp, c, w), row(3)),
                  pl.BlockSpec((1, w), lambda bi, j: (0, 0)), pl.BlockSpec((1, dv), lambda bi, j: (0, 0))],
        out_specs=[pl.BlockSpec((grp, c, w), lambda bi, j: (bi, j, 0)),
                   pl.BlockSpec((grp, heads, dk, dv), lambda bi, j: (bi, 0, 0, 0))],
        out_shape=[jax.ShapeDtypeStruct((nb, t, w), BF16),
                   jax.ShapeDtypeStruct((nb, heads, dk, dv), F32)],
        scratch_shapes=[pltpu.VMEM((grp, heads, dv, dk), F32)],
        compiler_params=_params("parallel", "arbitrary"), name="hgrn_prompt")(z3, z3, z3, z3, lb, norm_g)
    return o.reshape(nb * t, w), st


def _columns(x, n):
    w = x.shape[1]
    pad = jnp.concatenate([x, jnp.zeros((w - n, w), F32)], axis=0) if n < w else x
    return pad.T


def _step_heads(q_all, k_all, a_all, v_ref, g_ref, ng_ref, s_ref, o_ref, so_ref, *, heads, dk, dv, tb):
    for h in range(heads):
        ks = slice(h * dk, (h + 1) * dk)
        vs = slice(h * dv, (h + 1) * dv)
        qc, kc, ac = _columns(q_all[:, ks], tb), _columns(k_all[:, ks], tb), _columns(a_all[:, ks], tb)
        v = v_ref[:, vs]
        row = lax.broadcasted_iota(jnp.int32, (tb, dv), 0)
        o = jnp.zeros((tb, dv), F32)
        for n in range(tb):
            s_new = ac[:, n:n + 1] * s_ref[0, n, h] + kc[:, n:n + 1] * v[n:n + 1]
            so_ref[0, n, h] = s_new
            o = jnp.where(row == n, jnp.sum(qc[:, n:n + 1] * s_new, axis=0, keepdims=True), o)
        o_ref[:, vs] = _gated_rms(o, g_ref[:, vs], ng_ref[...]).astype(BF16)


def _gla_step_kernel(q_ref, k_ref, v_ref, g_ref, la_ref, ng_ref, s_ref, prev_ref, o_ref, so_ref,
                     *, heads, dk, dv, tb):
    del prev_ref
    _step_heads(q_ref[...] * dk ** -0.5, k_ref[...], jnp.exp(la_ref[...]), v_ref, g_ref, ng_ref, s_ref,
                o_ref, so_ref, heads=heads, dk=dk, dv=dv, tb=tb)


def _hgrn_step_kernel(q_ref, f_ref, v_ref, g_ref, lb_ref, ng_ref, s_ref, prev_ref, o_ref, so_ref,
                      *, heads, dk, dv, tb):
    del prev_ref
    q_all, k_all, lf = _hgrn_inputs(q_ref[...], f_ref[...], lb_ref[...])
    _step_heads(q_all, k_all, jnp.exp(lf), v_ref, g_ref, ng_ref, s_ref, o_ref, so_ref,
                heads=heads, dk=dk, dv=dv, tb=tb)


def _gla_step(z, la, norm_g, states, new_states, layer):
    m = z.shape[0]
    _, _, heads, dk, dv = states.shape
    tb = 16
    wk, wv = heads * dk, heads * dv
    row = lambda col: (lambda i: (i, col))
    st = pl.BlockSpec((1, tb, heads, dk, dv), lambda i: (layer, i, 0, 0, 0))
    kern = functools.partial(_gla_step_kernel, heads=heads, dk=dk, dv=dv, tb=tb)
    return pl.pallas_call(
        kern, grid=(m // tb,),
        in_specs=[pl.BlockSpec((tb, wk), row(0)), pl.BlockSpec((tb, wk), row(1)),
                  pl.BlockSpec((tb, wv), row(1)), pl.BlockSpec((tb, wv), row(2)),
                  pl.BlockSpec((tb, wk), row(0)), pl.BlockSpec((1, dv), lambda i: (0, 0)), st,
                  pl.BlockSpec(memory_space=pl.ANY)],
        out_specs=[pl.BlockSpec((tb, wv), row(0)), st],
        out_shape=[jax.ShapeDtypeStruct((m, wv), BF16), jax.ShapeDtypeStruct(states.shape, F32)],
        input_output_aliases={7: 1},
        compiler_params=_params("parallel"), name="gla_step")(z, z, z, z, la, norm_g, states, new_states)


def _hgrn_step(z, lb, norm_g, states, new_states, layer, col0):
    m = z.shape[0]
    _, _, heads, dk, dv = states.shape
    tb = 16
    w = heads * dk
    row = lambda col: (lambda i: (i, col0 + col))
    st = pl.BlockSpec((1, tb, heads, dk, dv), lambda i: (layer, i, 0, 0, 0))
    kern = functools.partial(_hgrn_step_kernel, heads=heads, dk=dk, dv=dv, tb=tb)
    return pl.pallas_call(
        kern, grid=(m // tb,),
        in_specs=[pl.BlockSpec((tb, w), row(0)), pl.BlockSpec((tb, w), row(1)),
                  pl.BlockSpec((tb, w), row(2)), pl.BlockSpec((tb, w), row(3)),
                  pl.BlockSpec((1, w), lambda i: (0, 0)), pl.BlockSpec((1, dv), lambda i: (0, 0)), st,
                  pl.BlockSpec(memory_space=pl.ANY)],
        out_specs=[pl.BlockSpec((tb, w), lambda i: (i, 0)), st],
        out_shape=[jax.ShapeDtypeStruct((m, w), BF16), jax.ShapeDtypeStruct(states.shape, F32)],
        input_output_aliases={7: 1},
        compiler_params=_params("parallel"), name="hgrn_step")(z, z, z, z, lb, norm_g, states, new_states)


def _taps(win, w_ref, lanes, first, n_taps):
    n = win.shape[0]
    acc = None
    for res in range(SUBLANES):
        offs = [o for o in range(first, first + n_taps) if o % SUBLANES == res]
        if not offs:
            continue
        sh = win if res == 0 else pltpu.roll(win, n - res, axis=0)
        for o in offs:
            base = o - res
            term = w_ref[o - first:o - first + 1, lanes] * sh[base:base + CONV_ROWS]
            acc = term if acc is None else acc + term
    return acc


def _conv_kernel(gb_ref, gc_ref, hb_ref, ga_ref, gs_ref, sw_ref, cw_ref, cb_ref, lg_ref, lb_ref,
                 ob_ref, oc_ref, so_ref, co_ref, ubuf, cbuf, *, tt, sw, cw):
    t = pl.program_id(1)

    @pl.when(t == 0)
    def _():
        ubuf[0:SCONV_HIST] = jnp.zeros((SCONV_HIST, ubuf.shape[1]), F32)
        cbuf[0:CONF_HIST] = jnp.zeros((CONF_HIST, cbuf.shape[1]), F32)

    ubuf[SCONV_HIST:SCONV_HIST + tt] = gc_ref[...] * hb_ref[...]
    cbuf[CONF_HIST:CONF_HIST + tt] = ga_ref[...] * _sigmoid(gs_ref[...])
    s0 = SCONV_HIST - (sw - 1)
    c0 = CONF_HIST - (cw - 1)

    def strip(r, carry):
        r0 = pl.multiple_of(r * CONV_ROWS, CONV_ROWS)
        yb, yc = [], []
        for lb in range(ubuf.shape[1] // LANES):
            ls = slice(lb * LANES, (lb + 1) * LANES)
            yb.append(_taps(ubuf[pl.ds(r0, CONV_ROWS + SCONV_HIST), ls], sw_ref, ls, s0, sw))
            yc.append(_taps(cbuf[pl.ds(r0, CONV_ROWS + CONF_HIST), ls], cw_ref, ls, c0, cw))
        acc = jnp.concatenate(yb, axis=1)
        ob_ref[pl.ds(r0, CONV_ROWS)] = (gb_ref[pl.ds(r0, CONV_ROWS)] * acc).astype(BF16)
        acc = jnp.concatenate(yc, axis=1)
        y = _layer_norm(acc + cb_ref[...], lg_ref[...], lb_ref[...])
        oc_ref[pl.ds(r0, CONV_ROWS)] = _silu(y).astype(BF16)
        return carry

    lax.fori_loop(0, tt // CONV_ROWS, strip, 0)

    @pl.when(t == pl.num_programs(1) - 1)
    def _():
        so_ref[0] = ubuf[SCONV_HIST + tt - (sw - 1):SCONV_HIST + tt]
        co_ref[0] = cbuf[CONF_HIST + tt - (cw - 1):CONF_HIST + tt]

    ubuf[0:SCONV_HIST] = ubuf[tt:tt + SCONV_HIST]
    cbuf[0:CONF_HIST] = cbuf[tt:tt + CONF_HIST]


def _conv_prompt(z, sconv_w, conf_w, conf_b, ln_g, ln_b, nb, t, d, col0):
    sw, cw = sconv_w.shape[0], conf_w.shape[0]
    tt = _tile(t, 256, CONF_HIST)
    n = t // tt
    row = lambda col: (lambda bi, j: (bi * n + j, col0 + col))
    vec = lambda r: pl.BlockSpec((r, d), lambda bi, j: (0, 0))
    kern = functools.partial(_conv_kernel, tt=tt, sw=sw, cw=cw)
    blk = lambda col: pl.BlockSpec((tt, d), row(col))
    return pl.pallas_call(
        kern, grid=(nb, n),
        in_specs=[blk(0), blk(1), blk(2), blk(3), blk(4), vec(sw), vec(cw), vec(1), vec(1), vec(1)],
        out_specs=[pl.BlockSpec((tt, d), lambda bi, j: (bi * n + j, 0)),
                   pl.BlockSpec((tt, d), lambda bi, j: (bi * n + j, 0)),
                   pl.BlockSpec((1, sw - 1, d), lambda bi, j: (bi, 0, 0)),
                   pl.BlockSpec((1, cw - 1, d), lambda bi, j: (bi, 0, 0))],
        out_shape=[jax.ShapeDtypeStruct((nb * t, d), BF16), jax.ShapeDtypeStruct((nb * t, d), BF16),
                   jax.ShapeDtypeStruct((nb, sw - 1, d), F32), jax.ShapeDtypeStruct((nb, cw - 1, d), F32)],
        scratch_shapes=[pltpu.VMEM((SCONV_HIST + tt, d), F32), pltpu.VMEM((CONF_HIST + tt, d), F32)],
        compiler_params=_params("parallel", "arbitrary"), name="conv_prompt")(
            z, z, z, z, z, sconv_w, conf_w, conf_b, ln_g, ln_b)


def _conv_step_kernel(gb_ref, gc_ref, hb_ref, ga_ref, gs_ref, sc_ref, cc_ref, sw_ref, cw_ref, cb_ref,
                      lg_ref, lb_ref, prev_s_ref, prev_c_ref, ob_ref, oc_ref, ns_ref, nc_ref, *, tb, sw, cw):
    del prev_s_ref, prev_c_ref
    u = gc_ref[...] * hb_ref[...]
    uc = ga_ref[...] * _sigmoid(gs_ref[...])
    row = lax.broadcasted_iota(jnp.int32, u.shape, 0)
    ys = jnp.zeros_like(u)
    yc = jnp.zeros_like(u)
    for n in range(tb):
        ys = jnp.where(row == n, jnp.sum(sc_ref[0, n] * sw_ref[0:sw - 1], axis=0, keepdims=True), ys)
        yc = jnp.where(row == n, jnp.sum(cc_ref[0, n] * cw_ref[0:cw - 1], axis=0, keepdims=True), yc)
        ns_ref[0, n, 0:sw - 2] = sc_ref[0, n, 1:sw - 1]
        ns_ref[0, n, sw - 2:sw - 1] = u[n:n + 1]
        nc_ref[0, n, 0:cw - 2] = cc_ref[0, n, 1:cw - 1]
        nc_ref[0, n, cw - 2:cw - 1] = uc[n:n + 1]
    yb = ys + sw_ref[sw - 1:sw] * u
    ob_ref[...] = (gb_ref[...] * yb).astype(BF16)
    y = yc + cw_ref[cw - 1:cw] * uc + cb_ref[...]
    oc_ref[...] = _silu(_layer_norm(y, lg_ref[...], lb_ref[...])).astype(BF16)


def _conv_step(z, caches_s, caches_c, new_s, new_c, layer, sconv_w, conf_w, conf_b, ln_g, ln_b, d, col0):
    m = z.shape[0]
    sw, cw = sconv_w.shape[0], conf_w.shape[0]
    tb = 16
    blk = lambda col: pl.BlockSpec((tb, d), lambda i: (i, col0 + col))
    vec = lambda r: pl.BlockSpec((r, d), lambda i: (0, 0))
    out = pl.BlockSpec((tb, d), lambda i: (i, 0))
    cs = pl.BlockSpec((1, tb, sw - 1, d), lambda i: (layer, i, 0, 0))
    cc = pl.BlockSpec((1, tb, cw - 1, d), lambda i: (layer, i, 0, 0))
    kern = functools.partial(_conv_step_kernel, tb=tb, sw=sw, cw=cw)
    return pl.pallas_call(
        kern, grid=(m // tb,),
        in_specs=[blk(0), blk(1), blk(2), blk(3), blk(4), cs, cc,
                  vec(sw), vec(cw), vec(1), vec(1), vec(1),
                  pl.BlockSpec(memory_space=pl.ANY), pl.BlockSpec(memory_space=pl.ANY)],
        out_specs=[out, out, cs, cc],
        out_shape=[jax.ShapeDtypeStruct((m, d), BF16), jax.ShapeDtypeStruct((m, d), BF16),
                   jax.ShapeDtypeStruct(caches_s.shape, F32), jax.ShapeDtypeStruct(caches_c.shape, F32)],
        input_output_aliases={12: 2, 13: 3},
        compiler_params=_params("parallel"), name="conv_step")(
            z, z, z, z, z, caches_s, caches_c, sconv_w, conf_w, conf_b, ln_g, ln_b, new_s, new_c)


def _merge_kernel(oa_ref, ob_ref, oc_ref, od_ref, g0_ref, g1_ref, g2_ref, g3_ref, h_ref,
                  wa_ref, wb_ref, wc_ref, wd_ref, wo_ref, lg_ref, lb_ref, prev_ref, o_ref, *, alpha):
    del prev_ref
    merged = _sigmoid(g0_ref[...]) * _dot(oa_ref[...], wa_ref[...])
    merged = merged + _sigmoid(g1_ref[...]) * _dot(ob_ref[...], wb_ref[...])
    merged = merged + _sigmoid(g2_ref[...]) * _dot(oc_ref[...], wc_ref[...])
    merged = merged + _sigmoid(g3_ref[...]) * _dot(od_ref[...], wd_ref[...])
    out = _dot(merged.astype(BF16), wo_ref[...])
    o_ref[...] = _layer_norm(alpha * h_ref[...] + out, lg_ref[...], lb_ref[...])


def _merge(oa, ob, oc, od, z, h, out, wa, wb, wc, wd, wo, ln_g, ln_b, alpha, gate_col0, row0):
    m, d = oa.shape
    tm = _tile(m, 256, 16)
    assert row0 % tm == 0
    r0 = row0 // tm
    row = pl.BlockSpec((tm, d), lambda i: (i, 0))
    mrow = pl.BlockSpec((tm, d), lambda i: (r0 + i, 0))
    gate = lambda c: pl.BlockSpec((tm, d), lambda i: (i, gate_col0 + c))
    wsp = pl.BlockSpec((d, d), lambda i: (0, 0))
    vec = pl.BlockSpec((1, d), lambda i: (0, 0))
    return pl.pallas_call(
        functools.partial(_merge_kernel, alpha=alpha), grid=(m // tm,),
        in_specs=[row, row, row, row, gate(0), gate(1), gate(2), gate(3), mrow, wsp, wsp, wsp, wsp, wsp, vec, vec,
                  pl.BlockSpec(memory_space=pl.ANY)],
        out_specs=mrow, out_shape=jax.ShapeDtypeStruct(out.shape, F32),
        input_output_aliases={16: 0},
        compiler_params=_params("parallel"), name="merge")(
            oa, ob, oc, od, z, z, z, z, h, wa, wb, wc, wd, wo, ln_g, ln_b, out)


def _router_kernel(x_ref, w_ref, b_ref, wt_ref, pos_ref, len_ref, off_ref, base_ref, tot_ref, carry_ref,
                   *, n_exp):
    i = pl.program_id(0)

    @pl.when(i == 0)
    def _():
        carry_ref[...] = jnp.zeros_like(carry_ref)

    tm = x_ref.shape[0]
    logits = _dot(x_ref[...].astype(BF16), w_ref[...].astype(BF16)) + b_ref[...]
    lane = lax.broadcasted_iota(jnp.int32, (tm, LANES), 1)
    lane_f = lane.astype(F32)
    work = jnp.where(lane < n_exp, logits, -jnp.inf)
    vals, hots = [], []
    for kk in range(TOP_K):
        m = jnp.max(work, axis=-1, keepdims=True)
        first = jnp.min(jnp.where(work == m, lane_f, float(LANES)), axis=-1, keepdims=True)
        hot = lane_f == first
        work = jnp.where(hot, -jnp.inf, work)
        vals.append(m)
        hots.append(hot)
    es = [jnp.exp(v - vals[0]) for v in vals]
    den = es[0]
    for e in es[1:]:
        den = den + e
    wt = jnp.zeros((tm, LANES), F32)
    for kk in range(TOP_K):
        wt = jnp.where(lane == kk, es[kk] / den, wt)
    chosen = hots[0]
    for hot in hots[1:]:
        chosen = chosen | hot
    onehot = jnp.where(chosen, 1.0, 0.0)
    r = lax.broadcasted_iota(jnp.int32, (tm, tm), 0)
    c = lax.broadcasted_iota(jnp.int32, (tm, tm), 1)
    before = jnp.where(c < r, 1.0, 0.0).astype(BF16)
    seen = _dot(before, onehot.astype(BF16))
    cnt = jnp.sum(onehot, axis=0, keepdims=True)
    groups = jnp.floor((cnt + (SUBLANES - 1)) * (1.0 / SUBLANES))
    er = lax.broadcasted_iota(jnp.int32, (LANES, LANES), 0)
    ec = lax.broadcasted_iota(jnp.int32, (LANES, LANES), 1)
    earlier = jnp.where(er < ec, 1.0, 0.0).astype(BF16)
    groups8 = jnp.broadcast_to(groups, (SUBLANES, LANES)).astype(BF16)
    toff = _dot(groups8, earlier)[0:1] * float(SUBLANES)
    cnt_pad = groups * float(SUBLANES)
    where_row = toff + seen
    pos = jnp.zeros((tm, LANES), F32)
    for kk in range(TOP_K):
        pk = jnp.sum(jnp.where(hots[kk], where_row, 0.0), axis=-1, keepdims=True)
        pos = jnp.where(lane == kk, pk, pos)
    wt_ref[...] = wt
    pos_ref[...] = pos.astype(jnp.int32)
    len_ref[0] = cnt_pad.astype(jnp.int32)
    off_ref[0] = toff.astype(jnp.int32)
    base_ref[0] = carry_ref[...].astype(jnp.int32)
    carry_ref[...] = carry_ref[...] + cnt_pad
    tot_ref[...] = carry_ref[...].astype(jnp.int32)


def _router(x, w_pad, b_pad, n_exp, tm):
    m, d = x.shape
    row = pl.BlockSpec((tm, LANES), lambda i: (i, 0))
    one = pl.BlockSpec((1, LANES), lambda i: (0, 0))
    meta = pl.BlockSpec((1, 1, LANES), lambda i: (i, 0, 0))
    n = m // tm
    meta_shape = jax.ShapeDtypeStruct((n, 1, LANES), jnp.int32)
    return pl.pallas_call(
        functools.partial(_router_kernel, n_exp=n_exp), grid=(n,),
        in_specs=[pl.BlockSpec((tm, d), lambda i: (i, 0)), pl.BlockSpec((d, LANES), lambda i: (0, 0)), one],
        out_specs=[row, row, meta, meta, meta, one],
        out_shape=[jax.ShapeDtypeStruct((m, LANES), F32), jax.ShapeDtypeStruct((m, LANES), jnp.int32),
                   meta_shape, meta_shape, meta_shape, jax.ShapeDtypeStruct((1, LANES), jnp.int32)],
        scratch_shapes=[pltpu.VMEM((1, LANES), F32)],
        compiler_params=_params("arbitrary"), name="router")(x, w_pad, b_pad)


def _dispatch_kernel(pend_ref, len_ref, off_ref, dst_ref, plen_ref, poff_ref, pdst_ref, x_ref, pos_ref, xs_ref,
                     zbuf, grouped, sem, zsem, *, n_exp, sizes):
    @pl.when(pl.program_id(0) == 0)
    def _():
        zbuf[...] = jnp.zeros_like(zbuf)

        def zero_copy(e):
            first = pl.multiple_of(pend_ref[e] - MOE_ROWS, MOE_ROWS)
            return pltpu.make_async_copy(zbuf, xs_ref.at[pl.ds(first, MOE_ROWS)], zsem)

        def nonempty(e):
            return pend_ref[e] > jnp.where(e > 0, pend_ref[jnp.maximum(e - 1, 0)], 0)

        def zstart(e, carry):
            @pl.when(nonempty(e))
            def _():
                zero_copy(e).start()
            return carry

        def zwait(e, carry):
            @pl.when(nonempty(e))
            def _():
                zero_copy(e).wait()
            return carry

        lax.fori_loop(0, n_exp, zstart, 0)
        lax.fori_loop(0, n_exp, zwait, 0)

        def tail_copy(b):
            return pltpu.make_async_copy(zbuf, xs_ref.at[pl.ds(pl.multiple_of(b * MOE_ROWS, MOE_ROWS), MOE_ROWS)], zsem)

        def tstart(b, carry):
            tail_copy(b).start()
            return carry

        def twait(b, carry):
            tail_copy(b).wait()
            return carry

        first_unused = pend_ref[n_exp - 1] // MOE_ROWS
        lax.fori_loop(first_unused, xs_ref.shape[0] // MOE_ROWS, tstart, 0)
        lax.fori_loop(first_unused, xs_ref.shape[0] // MOE_ROWS, twait, 0)

    i = pl.program_id(0)
    buf = i % 2
    tt = x_ref.shape[0]
    cap = grouped.shape[1]
    slot = lax.broadcasted_iota(jnp.int32, (tt, cap), 1)
    pos = pos_ref[...]
    placed = slot == pos[:, 0:1]
    for kk in range(1, TOP_K):
        placed = placed | (slot == pos[:, kk:kk + 1])
    grouped[buf] = _dot_tn(jnp.where(placed, 1.0, 0.0).astype(BF16), x_ref[...].astype(BF16))

    def piece(meta, b):
        lens, offs, dsts = meta

        def make(e, offset, size):
            src = pl.multiple_of(offs[e] + offset, SUBLANES)
            dst = pl.multiple_of(dsts[e] + offset, SUBLANES)
            return pltpu.make_async_copy(grouped.at[b, pl.ds(src, size)], xs_ref.at[pl.ds(dst, size)], sem.at[b])
        return lens, make

    @pl.when(i > 0)
    def _():
        lens, make = piece((plen_ref, poff_ref, pdst_ref), 1 - buf)
        _for_each_run_piece(lens, n_exp, sizes, lambda *a: make(*a).wait())

    lens, make = piece((len_ref, off_ref, dst_ref), buf)
    _for_each_run_piece(lens, n_exp, sizes, lambda *a: make(*a).start())

    @pl.when(i == pl.num_programs(0) - 1)
    def _():
        _for_each_run_piece(lens, n_exp, sizes, lambda *a: make(*a).wait())


def _run_sizes(tt):
    sizes = [SUBLANES]
    while sizes[-1] * 2 <= tt + SUBLANES - 1:
        sizes.append(sizes[-1] * 2)
    return tuple(reversed(sizes))


def _for_each_run_piece(len_ref, n_exp, sizes, fn):
    def body(e, carry):
        length = len_ref[e]
        for size in sizes:
            @pl.when((length & size) != 0)
            def _():
                fn(e, length & ~(2 * size - 1), size)
        return carry

    lax.fori_loop(0, n_exp, body, 0)


def _grouped_rows(tt, n_exp):
    cap = tt * TOP_K + n_exp * SUBLANES
    return -(-cap // LANES) * LANES


def _dispatch(x, pos, run_len, run_off, run_dst, p_end, rows, tt):
    m, d = x.shape
    n_exp = p_end.shape[0]
    meta = pl.BlockSpec((LANES,), lambda i, pe: (i,), memory_space=pltpu.SMEM)
    prev = pl.BlockSpec((LANES,), lambda i, pe: (jnp.maximum(i - 1, 0),), memory_space=pltpu.SMEM)
    grid_spec = pltpu.PrefetchScalarGridSpec(
        num_scalar_prefetch=1, grid=(m // tt,),
        in_specs=[meta, meta, meta, prev, prev, prev,
                  pl.BlockSpec((tt, d), lambda i, pe: (i, 0)),
                  pl.BlockSpec((tt, LANES), lambda i, pe: (i, 0))],
        out_specs=pl.BlockSpec(memory_space=pl.ANY),
        scratch_shapes=[pltpu.VMEM((MOE_ROWS, d), F32), pltpu.VMEM((2, _grouped_rows(tt, n_exp), d), F32),
                        pltpu.SemaphoreType.DMA((2,)), pltpu.SemaphoreType.DMA(())])
    return pl.pallas_call(
        functools.partial(_dispatch_kernel, n_exp=n_exp, sizes=_run_sizes(tt)), grid_spec=grid_spec,
        out_shape=jax.ShapeDtypeStruct((rows, d), F32),
        compiler_params=_params("arbitrary"), name="moe_dispatch")(
            p_end, run_len, run_off, run_dst, run_len, run_off, run_dst, x, pos)


def _expert_kernel(be_ref, nu_ref, xs_ref, wgu_ref, bgu_ref, wd_ref, bd_ref, y_ref, wgu_bf, wd_bf, *, d_ff):
    i = pl.program_id(0)
    prev = be_ref[jnp.maximum(i - 1, 0)]

    @pl.when((i == 0) | (be_ref[i] != prev))
    def _():
        wgu_bf[...] = wgu_ref[0, 0].astype(BF16)
        wd_bf[...] = wd_ref[0, 0].astype(BF16)

    @pl.when(i < nu_ref[0])
    def _():
        gu = _dot(xs_ref[...].astype(BF16), wgu_bf[...]) + bgu_ref[0]
        x_glu = jnp.minimum(gu[:, :d_ff], SWIGLU_LIMIT)
        x_lin = jnp.clip(gu[:, d_ff:], -SWIGLU_LIMIT, SWIGLU_LIMIT)
        act = x_glu * _sigmoid(SWIGLU_ALPHA * x_glu) * (x_lin + 1.0)
        y_ref[...] = _dot(act.astype(BF16), wd_bf[...]) + bd_ref[0]

    @pl.when(i >= nu_ref[0])
    def _():
        y_ref[...] = jnp.zeros_like(y_ref)


def _experts(xs, block_expert, n_used, w_gate_up, b_gate_up, w_down, b_down, layer):
    rows, d = xs.shape
    _, n_exp, _, ff2 = w_gate_up.shape
    d_ff = ff2 // 2
    n_blocks = rows // MOE_ROWS
    blk = lambda i, be, nu: (jnp.minimum(i, nu[0] - 1), 0)
    grid_spec = pltpu.PrefetchScalarGridSpec(
        num_scalar_prefetch=2, grid=(n_blocks,),
        in_specs=[pl.BlockSpec((MOE_ROWS, d), blk),
                  pl.BlockSpec((1, 1, d, ff2), lambda i, be, nu: (layer, be[i], 0, 0)),
                  pl.BlockSpec((1, 1, ff2), lambda i, be, nu: (be[i], 0, 0)),
                  pl.BlockSpec((1, 1, d_ff, d), lambda i, be, nu: (layer, be[i], 0, 0)),
                  pl.BlockSpec((1, 1, d), lambda i, be, nu: (be[i], 0, 0))],
        out_specs=pl.BlockSpec((MOE_ROWS, d), lambda i, be, nu: (i, 0)),
        scratch_shapes=[pltpu.VMEM((d, ff2), BF16), pltpu.VMEM((d_ff, d), BF16)])
    return pl.pallas_call(
        functools.partial(_expert_kernel, d_ff=d_ff), grid_spec=grid_spec,
        out_shape=jax.ShapeDtypeStruct((rows, d), F32),
        compiler_params=_params("arbitrary"), name="moe_experts")(
            block_expert, n_used, xs, w_gate_up, b_gate_up.reshape(n_exp, 1, ff2),
            w_down, b_down.reshape(n_exp, 1, d))


def _combine_kernel(len_ref, off_ref, dst_ref, nlen_ref, noff_ref, ndst_ref, pos_ref, wt_ref, h_ref, lg_ref, lb_ref,
                    y_ref, o_ref, ob_ref, rows, moe, sem, *, n_exp, sizes, alpha):
    i = pl.program_id(0)
    buf = i % 2

    def piece(meta, b):
        lens, offs, dsts = meta

        def make(e, offset, size):
            src = pl.multiple_of(dsts[e] + offset, SUBLANES)
            dst = pl.multiple_of(offs[e] + offset, SUBLANES)
            return pltpu.make_async_copy(y_ref.at[pl.ds(src, size)], rows.at[b, pl.ds(dst, size)], sem.at[b])
        return lens, make

    lens, make = piece((len_ref, off_ref, dst_ref), buf)

    @pl.when(i == 0)
    def _():
        _for_each_run_piece(lens, n_exp, sizes, lambda *a: make(*a).start())

    @pl.when(i + 1 < pl.num_programs(0))
    def _():
        nlens, nmake = piece((nlen_ref, noff_ref, ndst_ref), 1 - buf)
        _for_each_run_piece(nlens, n_exp, sizes, lambda *a: nmake(*a).start())

    _for_each_run_piece(lens, n_exp, sizes, lambda *a: make(*a).wait())

    def token(n, carry):
        acc = wt_ref[n * TOP_K] * rows[buf, pl.ds(pos_ref[n * TOP_K], 1), :]
        for kk in range(1, TOP_K):
            acc = acc + wt_ref[n * TOP_K + kk] * rows[buf, pl.ds(pos_ref[n * TOP_K + kk], 1), :]
        moe[pl.ds(n, 1), :] = acc
        return carry

    lax.fori_loop(0, h_ref.shape[0], token, 0, unroll=4)
    y = _layer_norm(alpha * h_ref[...] + moe[...], lg_ref[...], lb_ref[...])
    o_ref[...] = y
    ob_ref[...] = y.astype(BF16)


def _combine(y, pos_flat, wt_flat, run_len, run_off, run_dst, h, ln_g, ln_b, alpha, n_exp, tt):
    m, d = h.shape
    vec = pl.BlockSpec((1, d), lambda i: (0, 0))
    row = pl.BlockSpec((tt, d), lambda i: (i, 0))
    n_tiles = m // tt
    meta = pl.BlockSpec((LANES,), lambda i: (i,), memory_space=pltpu.SMEM)
    nxt = pl.BlockSpec((LANES,), lambda i: (jnp.minimum(i + 1, n_tiles - 1),), memory_space=pltpu.SMEM)
    per_tok = pl.BlockSpec((pos_flat.shape[0] // n_tiles,), lambda i: (i,), memory_space=pltpu.SMEM)
    return pl.pallas_call(
        functools.partial(_combine_kernel, n_exp=n_exp, sizes=_run_sizes(tt), alpha=alpha), grid=(n_tiles,),
        in_specs=[meta, meta, meta, nxt, nxt, nxt, per_tok, per_tok, row, vec, vec,
                  pl.BlockSpec(memory_space=pl.ANY)],
        out_specs=[row, row],
        out_shape=[jax.ShapeDtypeStruct((m, d), F32), jax.ShapeDtypeStruct((m, d), BF16)],
        scratch_shapes=[pltpu.VMEM((2, _grouped_rows(tt, n_exp), d), F32), pltpu.VMEM((tt, d), F32),
                        pltpu.SemaphoreType.DMA((2,))],
        compiler_params=_params("arbitrary"), name="moe_combine")(
            run_len, run_off, run_dst, run_len, run_off, run_dst, pos_flat, wt_flat, h, ln_g, ln_b, y)


def _moe(h, w, alpha, layer):
    m, d = h.shape
    n_exp = w["n_exp"]
    tt = _tile(m, 384, 16)
    n_tiles = m // tt
    wt, pos, run_len, run_off, run_base, totals = _router(h, w["w_router_pad"], w["b_router_pad"], n_exp, tt)
    seg = totals[0, :n_exp]
    padded = (seg + MOE_ROWS - 1) // MOE_ROWS * MOE_ROWS
    p_end = jnp.cumsum(padded).astype(jnp.int32)
    p_start = jnp.zeros((LANES,), jnp.int32).at[:n_exp].set(p_end - padded)
    run_dst = (run_base[:, 0, :] + p_start[None, :]).reshape(-1)
    run_len, run_off = run_len.reshape(-1), run_off.reshape(-1)
    n_blocks = -(-(m * TOP_K + n_tiles * n_exp * (SUBLANES - 1)) // MOE_ROWS) + n_exp
    block_start = jnp.arange(n_blocks, dtype=jnp.int32) * MOE_ROWS
    block_expert = jnp.minimum(jnp.sum(p_end[None, :] <= block_start[:, None], axis=1), n_exp - 1).astype(jnp.int32)
    n_used = p_end[-1:] // MOE_ROWS
    xs = _dispatch(h, pos, run_len, run_off, run_dst, p_end, n_blocks * MOE_ROWS, tt)
    y = _experts(xs, block_expert, n_used, w["w_gate_up"], w["b_gate_up"], w["w_down"], w["b_down"], layer)
    chunk = max(LANES, 1 << (tt * TOP_K - 1).bit_length())
    per_tile = lambda a: jnp.pad(a[:, :TOP_K].reshape(n_tiles, tt * TOP_K),
                                 ((0, 0), (0, chunk - tt * TOP_K))).reshape(-1)
    return _combine(y, per_tile(pos), per_tile(wt), run_len, run_off, run_dst,
                    h, w["ln2_g"], w["ln2_b"], alpha, n_exp, tt)


def _mixers(h, hb, h1, w, dims, alpha, states, layer, row0):
    nb, t, d, gh, gdk, gdv, hh, hdk, hdv = dims
    z = _matmul(hb, w["w_in_main"], layer, row0, nb * t)
    la = _gla_decay(hb, w["w_lra"], w["w_gla_lr"], w["b_gla_lr"], row0, nb * t)
    col_conv = (2 * gh * gdk + 2 * gh * gdv) // d
    col_hgrn = col_conv + 5
    col_gate = col_hgrn + 4
    if states is None:
        oa, s_gla = _gla_prompt(z, la, w["gla_norm_g"], nb, t, gh, gdk, gdv)
        od, s_hgrn = _hgrn_prompt(z, w["hgrn_lb"], w["hgrn_norm_g"], nb, t, hh, hdk, hdv, col_hgrn)
        ob, oc, c_s, c_c = _conv_prompt(z, w["sconv_w"], w["conf_conv_w"], w["conf_conv_b"],
                                        w["conf_ln_g"], w["conf_ln_b"], nb, t, d, col_conv)
    else:
        st_gla, st_hgrn, new_gla, new_hgrn, caches_s, caches_c, new_s, new_c = states
        oa, s_gla = _gla_step(z, la, w["gla_norm_g"], st_gla, new_gla, layer)
        od, s_hgrn = _hgrn_step(z, w["hgrn_lb"], w["hgrn_norm_g"], st_hgrn, new_hgrn, layer, col_hgrn)
        ob, oc, c_s, c_c = _conv_step(z, caches_s, caches_c, new_s, new_c, layer, w["sconv_w"], w["conf_conv_w"],
                                      w["conf_conv_b"], w["conf_ln_g"], w["conf_ln_b"], d, col_conv)
    h1 = _merge(oa, ob, oc, od, z, h, h1, w["w_br_a"], w["w_br_b"], w["w_br_c"], w["w_br_d"], w["w_o"],
                w["ln1_g"], w["ln1_b"], alpha, col_gate, row0)
    return h1, (s_gla, s_hgrn, c_s, c_c)


def kernel(x_prompt, x_sample, state_gla, state_hgrn, cache_sconv, cache_conformer, ln_in_g, ln_in_b, w_in, w_gla_lr, b_gla_lr, gla_norm_g, w_br_a, sconv_w, w_br_b, conf_conv_w, conf_conv_b, conf_ln_g, conf_ln_b, w_br_c, hgrn_lb_logits, hgrn_norm_g, w_br_d, w_o, ln1_g, ln1_b, w_router, b_router, w_gate_up, b_gate_up, w_down, b_down, ln2_g, ln2_b):
    depth, d, _ = w_in.shape
    _, _, gh, gdk, gdv = state_gla.shape
    _, _, hh, hdk, hdv = state_hgrn.shape
    rank = w_gla_lr.shape[1]
    n_exp = w_router.shape[2]
    alpha = (2 * depth) ** 0.25
    lra0 = 2 * gh * gdk + 2 * gh * gdv

    lb_p = jax.nn.softmax(hgrn_lb_logits.astype(F32), axis=0)
    hgrn_lb = jnp.cumsum(lb_p, axis=0) - lb_p[:1]
    row = lambda a: a.reshape(1, -1)
    w_in_main = jnp.concatenate([w_in[:, :, :lra0], w_in[:, :, lra0 + rank:]], axis=2).astype(BF16)
    layers = []
    for l in range(depth):
        w_lra = jnp.zeros((d, LANES), F32).at[:, :rank].set(w_in[l, :, lra0:lra0 + rank])
        w2 = jnp.zeros((LANES, gh * gdk), F32).at[:rank].set(w_gla_lr[l])
        layers.append({
            "w_in_main": w_in_main,
            "w_lra": w_lra.astype(BF16), "w_gla_lr": w2.astype(BF16), "b_gla_lr": row(b_gla_lr[l]),
            "gla_norm_g": row(gla_norm_g[l]), "hgrn_norm_g": row(hgrn_norm_g[l]), "hgrn_lb": row(hgrn_lb[l]),
            "sconv_w": sconv_w[l], "conf_conv_w": conf_conv_w[l], "conf_conv_b": row(conf_conv_b[l]),
            "conf_ln_g": row(conf_ln_g[l]), "conf_ln_b": row(conf_ln_b[l]),
            "w_br_a": w_br_a[l].astype(BF16), "w_br_b": w_br_b[l].astype(BF16),
            "w_br_c": w_br_c[l].astype(BF16), "w_br_d": w_br_d[l].astype(BF16), "w_o": w_o[l].astype(BF16),
            "ln1_g": row(ln1_g[l]), "ln1_b": row(ln1_b[l]), "ln2_g": row(ln2_g[l]), "ln2_b": row(ln2_b[l]),
            "n_exp": n_exp,
            "w_router_pad": jnp.zeros((d, LANES), F32).at[:, :n_exp].set(w_router[l]),
            "b_router_pad": jnp.zeros((1, LANES), F32).at[0, :n_exp].set(b_router[l]),
            "w_gate_up": w_gate_up, "b_gate_up": b_gate_up[l], "w_down": w_down, "b_down": b_down[l],
        })

    nbp, tp, _ = x_prompt.shape
    nbs, ts, _ = x_sample.shape
    assert ts == 1, "the sample group advances its states by exactly one token"
    dims_p = (nbp, tp, d, gh, gdk, gdv, hh, hdk, hdv)
    dims_s = (nbs, ts, d, gh, gdk, gdv, hh, hdk, hdv)
    mp, ms = nbp * tp, nbs * ts
    h, hb = _ln_call(jnp.concatenate([x_prompt.reshape(mp, d), x_sample.reshape(ms, d)], axis=0), ln_in_g, ln_in_b)
    prompt_states = ([], [], [], [])
    new_gla, new_hgrn = jnp.zeros_like(state_gla), jnp.zeros_like(state_hgrn)
    new_s, new_c = jnp.zeros_like(cache_sconv), jnp.zeros_like(cache_conformer)
    for l, w in enumerate(layers):
        h1 = jnp.zeros_like(h)
        h1, st_p = _mixers(h, hb, h1, w, dims_p, alpha, None, l, 0)
        h1, (new_gla, new_hgrn, new_s, new_c) = _mixers(
            h, hb, h1, w, dims_s, alpha,
            (state_gla, state_hgrn, new_gla, new_hgrn, cache_sconv, cache_conformer, new_s, new_c), l, mp)
        h, hb = _moe(h1, w, alpha, l)
        for acc, s in zip(prompt_states, st_p):
            acc.append(s)
    y_p, y_s = h[:mp].reshape(nbp, tp, d), h[mp:].reshape(nbs, ts, d)
    return ((y_p, y_s) + tuple(jnp.stack(a) for a in prompt_states) + (new_gla, new_hgrn, new_s, new_c))
```
